```python
import math
import jax, jax.numpy as jnp
from jax import lax
import numpy as np

D_MODEL = 2048
BATCH = 4
SEQ = 4096
DEPTH = 1

HEAD_DIM = 128
MIX_WIDTH = D_MODEL
DIFF_WIDTH = MIX_WIDTH // 4
DIL_WIDTH = MIX_WIDTH - DIFF_WIDTH
N_DIL_HEADS = DIL_WIDTH // HEAD_DIM
DIL_CONFIGS = ((128, 1), (512, 4), (2048, 16))
DIL_BLOCK = 64
DIFF_QK_DIM = 64
DIFF_V_DIM = 2 * DIFF_QK_DIM
N_DIFF_HEADS = DIFF_WIDTH // DIFF_V_DIM
DIFF_QK_WIDTH = N_DIFF_HEADS * 2 * DIFF_QK_DIM
IN_WIDTH = 3 * DIL_WIDTH + 2 * DIFF_QK_WIDTH + DIFF_WIDTH
Q_BLOCK = 128
ROPE_THETA = 500000.0
ROPE_FRACTION = 4
PEER_HEADS = 8
PEER_N_KEYS = 128
PEER_N_EXPERTS = PEER_N_KEYS * PEER_N_KEYS
PEER_KEY_DIM = 256
PEER_TOPK = 16
PEER_TOKEN_BLOCK = 64
LN_EPS = 1e-5
NEG_BIG = -1e30
DEEPNORM_ALPHA = (2.0 * DEPTH) ** 0.25
DEEPNORM_BETA = (8.0 * DEPTH) ** -0.25

kernel_name = "hybrid_dilated_diffattn_peer_encoder_layer"


def layer_norm(x, g, b):
    xf = x.astype(jnp.float32)
    mu = jnp.mean(xf, -1, keepdims=True)
    xc = xf - mu
    var = jnp.mean(xc * xc, -1, keepdims=True)
    return (xc * lax.rsqrt(var + LN_EPS) * g.astype(jnp.float32) + b.astype(jnp.float32)).astype(x.dtype)


def head_rms(t, g):
    return t * lax.rsqrt(jnp.mean(t * t, -1, keepdims=True) + LN_EPS) * g.astype(jnp.float32)


def rope_tables(seq, rot_dim):
    inv_freq = 1.0 / (ROPE_THETA ** (jnp.arange(0, rot_dim, 2, dtype=jnp.float32) / rot_dim))
    ang = jnp.arange(seq, dtype=jnp.float32)[:, None] * inv_freq[None, :]
    return jnp.cos(ang), jnp.sin(ang)


def partial_rope(t, cos, sin):
    half = cos.shape[-1]
    shp = (1, t.shape[1]) + (1,) * (t.ndim - 3) + (half,)
    c = cos.reshape(shp).astype(t.dtype)
    s = sin.reshape(shp).astype(t.dtype)
    t1 = t[..., :half]
    t2 = t[..., half:2 * half]
    return jnp.concatenate([t1 * c - t2 * s, t1 * s + t2 * c, t[..., 2 * half:]], axis=-1)


def dilated_window_attention(q, k, v, window, dilation):
    B, S, H, dh = q.shape
    L = S // dilation
    half = window // (2 * dilation)
    blk = DIL_BLOCK
    nb = -(-L // blk)
    Lp = nb * blk

    def strided(t):
        t = t.reshape(B, L, dilation, H, dh).transpose(0, 2, 1, 3, 4)
        return jnp.pad(t, ((0, 0), (0, 0), (0, Lp - L), (0, 0), (0, 0)))

    def neighbours(t):
        tp = jnp.pad(t, ((0, 0), (0, 0), (blk, blk), (0, 0), (0, 0))).reshape(B, dilation, nb + 2, blk, H, dh)
        return jnp.concatenate([tp[:, :, :-2], tp[:, :, 1:-1], tp[:, :, 2:]], axis=3)

    qb = strided(q).reshape(B, dilation, nb, blk, H, dh)
    kn = neighbours(strided(k))
    vn = neighbours(strided(v)).astype(jnp.float32)
    s = jnp.einsum('brnqhd,brnkhd->brnhqk', qb, kn).astype(jnp.float32) * (dh ** -0.5)
    qi = jnp.arange(blk)[:, None]
    kj = jnp.arange(3 * blk)[None, :]
    kpos = (jnp.arange(nb)[:, None, None] - 1) * blk + kj[None]
    mask = (jnp.abs(blk + qi - kj) <= half)[None] & (kpos >= 0) & (kpos < L)
    s = jnp.where(mask[None, None, :, None], s, NEG_BIG)
    m = jnp.max(s, -1, keepdims=True)
    p = jnp.exp(s - m)
    den = jnp.sum(p, -1)
    o = jnp.einsum('brnhqk,brnkhd->brnqhd', p, vn) / jnp.swapaxes(den, -1, -2)[..., None]
    lse = jnp.swapaxes(m[..., 0] + jnp.log(den), -1, -2)

    def unstride(t):
        t = t.reshape((B, dilation, Lp) + t.shape[4:])[:, :, :L]
        t = jnp.moveaxis(t, 1, 2)
        return t.reshape((B, S) + t.shape[3:])

    return unstride(o), unstride(lse)


def differential_attention(q, k, v, lam):
    B, S, H, _, dq = q.shape
    nqb = S // Q_BLOCK
    qb = jnp.moveaxis(q.reshape(B, nqb, Q_BLOCK, H, 2, dq), 1, 0)
    vf = v.astype(jnp.float32)
    scale = dq ** -0.5

    def one_block(qblk):
        s = jnp.einsum('bqhmd,bkhmd->bhmqk', qblk, k).astype(jnp.float32) * scale
        p = jax.nn.softmax(s, axis=-1)
        a = p[:, :, 0] - lam * p[:, :, 1]
        return jnp.einsum('bhqk,bkhd->bqhd', a, vf)

    o = lax.map(one_block, qb)
    return jnp.moveaxis(o, 0, 1).reshape(B, S, H, v.shape[-1])


def hybrid_mixer(h, w_in, dil_norm_g, lambda_q1, lambda_k1, lambda_q2, lambda_k2, subln_g, w_out,
                 lam_init, rope_dil, rope_diff):
    B, S, _ = h.shape
    proj = jnp.einsum('bsd,de->bse', h, w_in)
    splits = [DIL_WIDTH, 2 * DIL_WIDTH, 3 * DIL_WIDTH,
              3 * DIL_WIDTH + DIFF_QK_WIDTH, 3 * DIL_WIDTH + 2 * DIFF_QK_WIDTH]
    qa, ka, va, qd, kd, vd = jnp.split(proj, splits, axis=-1)

    qa = partial_rope(qa.reshape(B, S, N_DIL_HEADS, HEAD_DIM), *rope_dil)
    ka = partial_rope(ka.reshape(B, S, N_DIL_HEADS, HEAD_DIM), *rope_dil)
    va = va.reshape(B, S, N_DIL_HEADS, HEAD_DIM)
    outs, lses = [], []
    for window, dilation in DIL_CONFIGS:
        o, l = dilated_window_attention(qa, ka, va, window, dilation)
        outs.append(o)
        lses.append(l)
    wts = jax.nn.softmax(jnp.stack(lses), axis=0)
    dil_out = jnp.sum(wts[..., None] * jnp.stack(outs), axis=0)
    dil_out = head_rms(dil_out, dil_norm_g).reshape(B, S, DIL_WIDTH)

    qd = partial_rope(qd.reshape(B, S, N_DIFF_HEADS, 2, DIFF_QK_DIM), *rope_diff)
    kd = partial_rope(kd.reshape(B, S, N_DIFF_HEADS, 2, DIFF_QK_DIM), *rope_diff)
    vd = vd.reshape(B, S, N_DIFF_HEADS, DIFF_V_DIM)
    lam = (jnp.exp(jnp.sum(lambda_q1.astype(jnp.float32) * lambda_k1.astype(jnp.float32)))
           - jnp.exp(jnp.sum(lambda_q2.astype(jnp.float32) * lambda_k2.astype(jnp.float32)))
           + lam_init)
    od = differential_attention(qd, kd, vd, lam)
    od = head_rms(od, subln_g) * (1.0 - lam_init)

    mixed = jnp.concatenate([dil_out, od.reshape(B, S, DIFF_WIDTH)], axis=-1).astype(h.dtype)
    return jnp.einsum('bse,ed->bsd', mixed, w_out)


def peer_ffn(h, w_query, sub_keys, expert_u, expert_v):
    B, S, D = h.shape
    T = B * S
    xt = h.reshape(T, D)
    q = jnp.einsum('td,de->te', xt, w_query).reshape(T, PEER_HEADS, 2, PEER_KEY_DIM // 2)
    s = jnp.einsum('thcd,hcnd->thcn', q, sub_keys).astype(jnp.float32)
    sv, si = lax.top_k(s, PEER_TOPK)
    cand = (sv[:, :, 0, :, None] + sv[:, :, 1, None, :]).reshape(T, PEER_HEADS, PEER_TOPK * PEER_TOPK)
    cand_idx = (si[:, :, 0, :, None] * PEER_N_KEYS + si[:, :, 1, None, :]).reshape(T, PEER_HEADS, PEER_TOPK * PEER_TOPK)
    top_s, pos = lax.top_k(cand, PEER_TOPK)
    idx = jnp.take_along_axis(cand_idx, pos, axis=-1)
    g = jax.nn.softmax(top_s, axis=-1)

    nblk = T // PEER_TOKEN_BLOCK
    xs = xt.reshape(nblk, PEER_TOKEN_BLOCK, D)
    idxs = idx.reshape(nblk, PEER_TOKEN_BLOCK, PEER_HEADS * PEER_TOPK)
    gs = g.reshape(nblk, PEER_TOKEN_BLOCK, PEER_HEADS * PEER_TOPK)

    def block(args):
        xb, ib, gb = args
        u = expert_u[ib]
        a = gb.astype(xb.dtype) * jax.nn.gelu(jnp.einsum('tkd,td->tk', u, xb), approximate=False)
        return jnp.einsum('tk,tkd->td', a, expert_v[ib])

    y = lax.map(block, (xs, idxs, gs))
    return y.reshape(B, S, D)


def setup_inputs(seed: int = 0) -> dict:
    key = jax.random.key(seed)
    ks = jax.random.split(key, 20)
    f32 = jnp.float32

    def nrm(k, shape, scale):
        return jax.random.normal(k, shape, f32) * scale

    col_scale = jnp.concatenate([
        jnp.ones((2 * DIL_WIDTH,), f32), jnp.full((DIL_WIDTH,), DEEPNORM_BETA, f32),
        jnp.ones((2 * DIFF_QK_WIDTH,), f32), jnp.full((DIFF_WIDTH,), DEEPNORM_BETA, f32)])
    return {
        "x": nrm(ks[0], (BATCH, SEQ, D_MODEL), 1.0),
        "ln_emb_g": 1.0 + nrm(ks[1], (D_MODEL,), 0.02),
        "ln_emb_b": nrm(ks[2], (D_MODEL,), 0.02),
        "w_in": nrm(ks[3], (DEPTH, D_MODEL, IN_WIDTH), D_MODEL ** -0.5) * col_scale,
        "dil_norm_g": 1.0 + nrm(ks[4], (DEPTH, HEAD_DIM), 0.02),
        "lambda_q1": nrm(ks[5], (DEPTH, DIFF_QK_DIM), 0.1),
        "lambda_k1": nrm(ks[6], (DEPTH, DIFF_QK_DIM), 0.1),
        "lambda_q2": nrm(ks[7], (DEPTH, DIFF_QK_DIM), 0.1),
        "lambda_k2": nrm(ks[8], (DEPTH, DIFF_QK_DIM), 0.1),
        "subln_g": 1.0 + nrm(ks[9], (DEPTH, DIFF_V_DIM), 0.02),
        "w_out": nrm(ks[10], (DEPTH, MIX_WIDTH, D_MODEL), MIX_WIDTH ** -0.5 * DEEPNORM_BETA),
        "ln1_g": 1.0 + nrm(ks[11], (DEPTH, D_MODEL), 0.02),
        "ln1_b": nrm(ks[12], (DEPTH, D_MODEL), 0.02),
        "peer_w_query": nrm(ks[13], (DEPTH, D_MODEL, PEER_HEADS * PEER_KEY_DIM), D_MODEL ** -0.5),
        "peer_sub_keys": nrm(ks[14], (DEPTH, PEER_HEADS, 2, PEER_N_KEYS, PEER_KEY_DIM // 2), (PEER_KEY_DIM // 2) ** -0.5),
        "peer_u": nrm(ks[15], (DEPTH, PEER_N_EXPERTS, D_MODEL), D_MODEL ** -0.5),
        "peer_v": nrm(ks[16], (DEPTH, PEER_N_EXPERTS, D_MODEL), DEEPNORM_BETA),
        "ln2_g": 1.0 + nrm(ks[17], (DEPTH, D_MODEL), 0.02),
        "ln2_b": nrm(ks[18], (DEPTH, D_MODEL), 0.02),
    }


def reference(x, ln_emb_g, ln_emb_b, w_in, dil_norm_g, lambda_q1, lambda_k1, lambda_q2, lambda_k2,
              subln_g, w_out, ln1_g, ln1_b, peer_w_query, peer_sub_keys, peer_u, peer_v, ln2_g, ln2_b):
    S = x.shape[1]
    rope_dil = rope_tables(S, HEAD_DIM // ROPE_FRACTION)
    rope_diff = rope_tables(S, DIFF_QK_DIM // ROPE_FRACTION)
    h = layer_norm(x, ln_emb_g, ln_emb_b)
    for l in range(DEPTH):
        lam_init = 0.8 - 0.6 * math.exp(-0.3 * l)
        mix = hybrid_mixer(h, w_in[l], dil_norm_g[l], lambda_q1[l], lambda_k1[l], lambda_q2[l], lambda_k2[l],
                           subln_g[l], w_out[l], lam_init, rope_dil, rope_diff)
        h = layer_norm(DEEPNORM_ALPHA * h + mix, ln1_g[l], ln1_b[l])
        ffn = peer_ffn(h, peer_w_query[l], peer_sub_keys[l], peer_u[l], peer_v[l])
        h = layer_norm(DEEPNORM_ALPHA * h + ffn, ln2_g[l], ln2_b[l])
    return h
```

```python
import functools
import math

import jax
import jax.numpy as jnp
from jax import lax
from jax.experimental import pallas as pl
from jax.experimental.pallas import tpu as pltpu

F32 = jnp.float32
BF16 = jnp.bfloat16

D_MODEL = 2048
DEPTH = 1
HEAD_DIM = 128
DIFF_WIDTH = D_MODEL // 4
DIL_WIDTH = D_MODEL - DIFF_WIDTH
N_DIL_HEADS = DIL_WIDTH // HEAD_DIM
DIL_CONFIGS = ((128, 1), (512, 4), (2048, 16))
DIFF_QK_DIM = 64
DIFF_V_DIM = 2 * DIFF_QK_DIM
N_DIFF_HEADS = DIFF_WIDTH // DIFF_V_DIM
DIFF_QK_WIDTH = N_DIFF_HEADS * 2 * DIFF_QK_DIM
IN_WIDTH = 3 * DIL_WIDTH + 2 * DIFF_QK_WIDTH + DIFF_WIDTH
ROPE_THETA = 500000.0
ROPE_FRACTION = 4
PEER_HEADS = 8
PEER_N_KEYS = 128
PEER_N_EXPERTS = PEER_N_KEYS * PEER_N_KEYS
PEER_KEY_DIM = 256
PEER_TOPK = 16
LN_EPS = 1e-5
NEG_BIG = -1e30
DEEPNORM_ALPHA = (2.0 * DEPTH) ** 0.25

LANES = 128
VMEM_LIMIT = 56 * 1024 * 1024

COL_QA = 0
COL_KA = DIL_WIDTH // LANES
COL_VA = 2 * DIL_WIDTH // LANES
COL_QD = 3 * DIL_WIDTH // LANES
COL_KD = COL_QD + DIFF_QK_WIDTH // LANES
COL_VD = COL_KD + DIFF_QK_WIDTH // LANES


def _cparams(*sem):
    return pltpu.CompilerParams(dimension_semantics=sem, vmem_limit_bytes=VMEM_LIMIT)


def _layer_norm(x, g, b):
    mu = jnp.mean(x, -1, keepdims=True)
    xc = x - mu
    var = jnp.mean(xc * xc, -1, keepdims=True)
    return xc * lax.rsqrt(var + LN_EPS) * g + b


def _head_rms(t, g):
    return t * lax.rsqrt(jnp.mean(t * t, -1, keepdims=True) + LN_EPS) * g


def _dot_nt(a, b):
    return lax.dot_general(a, b, (((1,), (1,)), ((), ())), preferred_element_type=F32)


INPROJ_TM = 512
INPROJ_TN = 512
ROPE_DIL_SHIFT = HEAD_DIM // ROPE_FRACTION // 2
ROPE_DIFF_SHIFT = DIFF_QK_DIM // ROPE_FRACTION // 2


def _rope_tables(seq, period, half):
    rot = 2 * half
    inv_freq = 1.0 / (ROPE_THETA ** (jnp.arange(0, rot, 2, dtype=F32) / rot))
    ang = jnp.arange(seq, dtype=F32)[:, None] * inv_freq[None, :]
    cos, sin = jnp.cos(ang), jnp.sin(ang)
    pad = period - rot
    c = jnp.concatenate([cos, cos, jnp.ones((seq, pad), F32)], axis=1)
    a = jnp.concatenate([-sin, jnp.zeros((seq, half + pad), F32)], axis=1)
    b = jnp.concatenate([jnp.zeros((seq, half), F32), sin, jnp.zeros((seq, pad), F32)], axis=1)
    reps = LANES // period
    return tuple(jnp.tile(t, (1, reps)) for t in (c, a, b))


def _inproj_kernel(x_ref, g_ref, b_ref, w_ref, c16_ref, a16_ref, b16_ref, c8_ref, a8_ref, b8_ref,
                   h_ref, o_ref, xs_ref, *, n_dil_blocks, diff_lo, diff_hi):
    j = pl.program_id(1)

    @pl.when(j == 0)
    def _():
        h = _layer_norm(x_ref[...], g_ref[...], b_ref[...])
        h_ref[...] = h
        xs_ref[...] = h.astype(BF16)

    acc = jnp.dot(xs_ref[...], w_ref[...], preferred_element_type=F32)
    groups = acc.shape[1] // LANES

    def rope(shift, c_ref, a_ref, b_ref):
        c, a, b = c_ref[...], a_ref[...], b_ref[...]
        for gidx in range(groups):
            t = acc[:, gidx * LANES:(gidx + 1) * LANES]
            r = t * c + pltpu.roll(t, LANES - shift, 1) * a + pltpu.roll(t, shift, 1) * b
            o_ref[:, gidx * LANES:(gidx + 1) * LANES] = r.astype(o_ref.dtype)

    is_dil = j < n_dil_blocks
    is_diff = jnp.logical_and(j >= diff_lo, j < diff_hi)

    @pl.when(is_dil)
    def _():
        rope(ROPE_DIL_SHIFT, c16_ref, a16_ref, b16_ref)

    @pl.when(is_diff)
    def _():
        rope(ROPE_DIFF_SHIFT, c8_ref, a8_ref, b8_ref)

    @pl.when(jnp.logical_not(jnp.logical_or(is_dil, is_diff)))
    def _():
        o_ref[...] = acc.astype(o_ref.dtype)


def _inproj(x2, g, b, w16, seq):
    t = x2.shape[0]
    tm, tn = INPROJ_TM, INPROJ_TN
    assert t % tm == 0 and seq % tm == 0 and IN_WIDTH % tn == 0
    assert (2 * DIL_WIDTH) % tn == 0 and (3 * DIL_WIDTH) % tn == 0 and DIFF_QK_WIDTH * 2 % tn == 0
    sblocks = seq // tm
    tabs16 = _rope_tables(seq, HEAD_DIM, ROPE_DIL_SHIFT)
    tabs8 = _rope_tables(seq, DIFF_QK_DIM, ROPE_DIFF_SHIFT)
    tab_spec = pl.BlockSpec((tm, LANES), lambda i, j: (i % sblocks, 0))
    kern = functools.partial(
        _inproj_kernel, n_dil_blocks=2 * DIL_WIDTH // tn,
        diff_lo=3 * DIL_WIDTH // tn, diff_hi=(3 * DIL_WIDTH + 2 * DIFF_QK_WIDTH) // tn)
    return pl.pallas_call(
        kern,
        grid=(t // tm, IN_WIDTH // tn),
        in_specs=[
            pl.BlockSpec((tm, D_MODEL), lambda i, j: (i, 0)),
            pl.BlockSpec((1, D_MODEL), lambda i, j: (0, 0)),
            pl.BlockSpec((1, D_MODEL), lambda i, j: (0, 0)),
            pl.BlockSpec((D_MODEL, tn), lambda i, j: (0, j)),
            tab_spec, tab_spec, tab_spec, tab_spec, tab_spec, tab_spec,
        ],
        out_specs=[
            pl.BlockSpec((tm, D_MODEL), lambda i, j: (i, 0)),
            pl.BlockSpec((tm, tn), lambda i, j: (i, j)),
        ],
        out_shape=[
            jax.ShapeDtypeStruct((t, D_MODEL), F32),
            jax.ShapeDtypeStruct((t, IN_WIDTH), BF16),
        ],
        scratch_shapes=[pltpu.VMEM((tm, D_MODEL), BF16)],
        compiler_params=_cparams("parallel", "arbitrary"),
        name="inproj",
    )(x2, g, b, w16, *tabs16, *tabs8)


DIL_BQ = 128
DIL_WIN = 256


def _dil_attn_kernel(q_ref, k_ref, v_ref, o_ref, lse_ref, *, length, half, bq, win):
    scale = HEAD_DIM ** -0.5

    def body(n, carry):
        q0 = pl.multiple_of(n * bq, bq)
        k0 = jnp.clip(q0 - half, 0, length - win)
        k0 = pl.multiple_of(k0, 16)
        q = q_ref[pl.ds(q0, bq), :]
        k = k_ref[pl.ds(k0, win), :]
        v = v_ref[pl.ds(k0, win), :]
        s = _dot_nt(q, k) * scale
        qpos = q0 + lax.broadcasted_iota(jnp.int32, (bq, win), 0)
        kpos = k0 + lax.broadcasted_iota(jnp.int32, (bq, win), 1)
        s = jnp.where(jnp.abs(qpos - kpos) <= half, s, NEG_BIG)
        m = jnp.max(s, -1, keepdims=True)
        p = jnp.exp(s - m)
        den = jnp.sum(p, -1, keepdims=True)
        o = jnp.dot(p.astype(BF16), v, preferred_element_type=F32) / den
        o_ref[pl.ds(q0, bq), :] = o
        lse_ref[pl.ds(q0, bq), :] = jnp.broadcast_to(m + jnp.log(den), (bq, LANES))
        return carry

    lax.fori_loop(0, length // bq, body, 0)


def _dil_attn(proj3, window, dilation):
    bsz, seq, _ = proj3.shape
    length = seq // dilation
    half = window // (2 * dilation)
    bq = min(DIL_BQ, length)
    win = min(DIL_WIN, length)
    assert seq % dilation == 0 and length % bq == 0 and half % 16 == 0 and win >= min(length, bq + 2 * half)
    p4 = proj3.reshape(bsz, length, dilation * IN_WIDTH)
    in_blocks, out_blocks = IN_WIDTH // LANES, DIL_WIDTH // LANES

    def spec(col0):
        return pl.BlockSpec((None, length, LANES), lambda b, h, r: (b, 0, r * in_blocks + col0 + h))

    out_spec = pl.BlockSpec((None, length, LANES), lambda b, h, r: (b, 0, r * out_blocks + h))
    out_sds = jax.ShapeDtypeStruct((bsz, length, dilation * DIL_WIDTH), F32)
    o, lse = pl.pallas_call(
        functools.partial(_dil_attn_kernel, length=length, half=half, bq=bq, win=win),
        grid=(bsz, N_DIL_HEADS, dilation),
        in_specs=[spec(COL_QA), spec(COL_KA), spec(COL_VA)],
        out_specs=[out_spec, out_spec],
        out_shape=[out_sds, out_sds],
        compiler_params=_cparams("parallel", "parallel", "parallel"),
        name=f"dil_attn_d{dilation}",
    )(p4, p4, p4)
    return o.reshape(bsz * seq, DIL_WIDTH), lse.reshape(bsz * seq, DIL_WIDTH)


MIX_TM = 1024


def _dil_mix_kernel(o1, o2, o3, l1, l2, l3, g_ref, out_ref):
    la, lb, lc = l1[...], l2[...], l3[...]
    m = jnp.maximum(jnp.maximum(la, lb), lc)
    wa, wb, wc = jnp.exp(la - m), jnp.exp(lb - m), jnp.exp(lc - m)
    mixed = (wa * o1[...] + wb * o2[...] + wc * o3[...]) / (wa + wb + wc)
    out_ref[...] = _head_rms(mixed, g_ref[...]).astype(out_ref.dtype)


def _dil_mix(outs, lses, g):
    t = outs[0].shape[0]
    tm = MIX_TM
    assert t % tm == 0
    spec = pl.BlockSpec((tm, LANES), lambda i, h: (i, h))
    return pl.pallas_call(
        _dil_mix_kernel,
        grid=(t // tm, N_DIL_HEADS),
        in_specs=[spec] * 6 + [pl.BlockSpec((1, LANES), lambda i, h: (0, 0))],
        out_specs=spec,
        out_shape=jax.ShapeDtypeStruct((t, DIL_WIDTH), BF16),
        compiler_params=_cparams("parallel", "parallel"),
        name="dil_mix",
    )(*outs, *lses, g)


DIFF_TQ = 256


def _diff_attn_kernel(q_ref, k_ref, v_ref, lq1, lk1, lq2, lk2, g_ref, o_ref, *, lam_init):
    scale = DIFF_QK_DIM ** -0.5
    q = q_ref[...]
    k = k_ref[...]
    v = v_ref[...]
    lo = lax.broadcasted_iota(jnp.int32, q.shape, 1) < DIFF_QK_DIM
    zero = jnp.zeros_like(q)

    def one_map(qm):
        s = _dot_nt(qm, k) * scale
        m = jnp.max(s, -1, keepdims=True)
        p = jnp.exp(s - m)
        den = jnp.sum(p, -1, keepdims=True)
        return jnp.dot(p.astype(BF16), v, preferred_element_type=F32) / den

    o0 = one_map(jnp.where(lo, q, zero))
    o1 = one_map(jnp.where(lo, zero, q))
    lam = (jnp.exp(jnp.sum(lq1[...] * lk1[...], keepdims=True))
           - jnp.exp(jnp.sum(lq2[...] * lk2[...], keepdims=True)) + lam_init)
    a = o0 - lam * o1
    o_ref[...] = (_head_rms(a, g_ref[...]) * (1.0 - lam_init)).astype(o_ref.dtype)


def _diff_attn(proj3, lq1, lk1, lq2, lk2, g, lam_init):
    bsz, seq, _ = proj3.shape
    tq = DIFF_TQ
    assert seq % tq == 0
    vec = pl.BlockSpec((1, DIFF_QK_DIM), lambda b, h, i: (0, 0))
    return pl.pallas_call(
        functools.partial(_diff_attn_kernel, lam_init=lam_init),
        grid=(bsz, N_DIFF_HEADS, seq // tq),
        in_specs=[
            pl.BlockSpec((None, tq, LANES), lambda b, h, i: (b, i, COL_QD + h)),
            pl.BlockSpec((None, seq, LANES), lambda b, h, i: (b, 0, COL_KD + h)),
            pl.BlockSpec((None, seq, LANES), lambda b, h, i: (b, 0, COL_VD + h)),
            vec, vec, vec, vec,
            pl.BlockSpec((1, DIFF_V_DIM), lambda b, h, i: (0, 0)),
        ],
        out_specs=pl.BlockSpec((None, tq, LANES), lambda b, h, i: (b, i, h)),
        out_shape=jax.ShapeDtypeStruct((bsz, seq, DIFF_WIDTH), BF16),
        compiler_params=_cparams("parallel", "parallel", "arbitrary"),
        name="diff_attn",
    )(proj3, proj3, proj3, lq1, lk1, lq2, lk2, g)


OUTPROJ_TM = 256


def _outproj_kernel(ma_ref, md_ref, wa_ref, wd_ref, h_ref, g_ref, b_ref, o32_ref, o16_ref):
    mix = jnp.dot(ma_ref[...], wa_ref[...], preferred_element_type=F32)
    mix = mix + jnp.dot(md_ref[...], wd_ref[...], preferred_element_type=F32)
    h = _layer_norm(DEEPNORM_ALPHA * h_ref[...] + mix, g_ref[...], b_ref[...])
    o32_ref[...] = h
    o16_ref[...] = h.astype(BF16)


def _outproj(mix_a, mix_d, w_a, w_d, h0, g, b):
    t = h0.shape[0]
    tm = OUTPROJ_TM
    assert t % tm == 0
    row = pl.BlockSpec((1, D_MODEL), lambda i: (0, 0))
    return pl.pallas_call(
        _outproj_kernel,
        grid=(t // tm,),
        in_specs=[
            pl.BlockSpec((tm, DIL_WIDTH), lambda i: (i, 0)),
            pl.BlockSpec((tm, DIFF_WIDTH), lambda i: (i, 0)),
            pl.BlockSpec((DIL_WIDTH, D_MODEL), lambda i: (0, 0)),
            pl.BlockSpec((DIFF_WIDTH, D_MODEL), lambda i: (0, 0)),
            pl.BlockSpec((tm, D_MODEL), lambda i: (i, 0)),
            row, row,
        ],
        out_specs=[pl.BlockSpec((tm, D_MODEL), lambda i: (i, 0))] * 2,
        out_shape=[jax.ShapeDtypeStruct((t, D_MODEL), F32), jax.ShapeDtypeStruct((t, D_MODEL), BF16)],
        compiler_params=_cparams("parallel"),
        name="outproj",
    )(mix_a, mix_d, w_a, w_d, h0, g, b)


SCORE_TM = 512
N_KEYSETS = 2 * PEER_HEADS
HALF_KEY_DIM = PEER_KEY_DIM // 2


def _peer_score_kernel(h_ref, wq_ref, keys_ref, s_ref):
    q = jnp.dot(h_ref[...], wq_ref[...], preferred_element_type=F32).astype(BF16)
    for hc in range(N_KEYSETS):
        qs = q[:, hc * HALF_KEY_DIM:(hc + 1) * HALF_KEY_DIM]
        s_ref[hc] = _dot_nt(keys_ref[hc], qs)


def _peer_scores(h16, wq16, keys16):
    t = h16.shape[0]
    tm = SCORE_TM
    assert t % tm == 0 and HALF_KEY_DIM == LANES
    return pl.pallas_call(
        _peer_score_kernel,
        grid=(t // tm,),
        in_specs=[
            pl.BlockSpec((tm, D_MODEL), lambda i: (i, 0)),
            pl.BlockSpec((D_MODEL, PEER_HEADS * PEER_KEY_DIM), lambda i: (0, 0)),
            pl.BlockSpec((N_KEYSETS, PEER_N_KEYS, HALF_KEY_DIM), lambda i: (0, 0, 0)),
        ],
        out_specs=pl.BlockSpec((N_KEYSETS, PEER_N_KEYS, tm), lambda i: (0, 0, i)),
        out_shape=jax.ShapeDtypeStruct((N_KEYSETS, PEER_N_KEYS, t), F32),
        compiler_params=_cparams("parallel"),
        name="peer_scores",
    )(h16, wq16, keys16)


ROUTE_TT = 256


def _top_ranks(v, k):
    n = v.shape[0]
    rows = lax.broadcasted_iota(jnp.int32, v.shape, 0)
    rank = jnp.full(v.shape, k, jnp.int32)
    work = v
    vals = []
    for r in range(k):
        m = jnp.max(work, axis=0, keepdims=True)
        first = jnp.min(jnp.where(work == m, rows, n), axis=0, keepdims=True)
        sel = rows == first
        rank = jnp.where(sel, r, rank)
        work = jnp.where(sel, -jnp.inf, work)
        vals.append(m)
    return jnp.concatenate(vals, axis=0), rank


def _peer_route_kernel(s_ref, n_ref, e0_ref, r1_ref, e1_ref):
    k = PEER_TOPK
    s0 = s_ref[0]
    s1 = s_ref[1]
    tt = s0.shape[1]
    sv0, rank0 = _top_ranks(s0, k)
    sv1, rank1 = _top_ranks(s1, k)
    cand = (sv0[:, None, :] + sv1[None, :, :]).reshape(k * k, tt)
    _, crank = _top_ranks(cand, k)
    chosen = (crank < k).reshape(k, k, tt)
    e0s = jnp.exp(sv0 - sv0[0:1])
    e1s = jnp.exp(sv1 - sv1[0:1])
    gates = jnp.where(chosen, e0s[:, None, :] * e1s[None, :, :], 0.0)
    z = jnp.sum(jnp.sum(gates, axis=1), axis=0, keepdims=True)
    count = jnp.sum(chosen.astype(F32), axis=1)
    n_of_row = jnp.zeros(s0.shape, F32)
    for a in range(k):
        n_of_row = jnp.where(rank0 == a, count[a:a + 1], n_of_row)
    n_ref[...] = n_of_row
    e0_ref[...] = jnp.exp(s0 - sv0[0:1])
    r1_ref[...] = rank1.astype(F32)
    e1_ref[...] = jnp.exp(s1 - sv1[0:1]) / z


def _peer_route(scores):
    t = scores.shape[-1]
    tt = ROUTE_TT
    assert t % tt == 0
    out_spec = pl.BlockSpec((None, PEER_N_KEYS, tt), lambda h, i: (h, 0, i))
    out_sds = jax.ShapeDtypeStruct((PEER_HEADS, PEER_N_KEYS, t), F32)
    return pl.pallas_call(
        _peer_route_kernel,
        grid=(PEER_HEADS, t // tt),
        in_specs=[pl.BlockSpec((2, PEER_N_KEYS, tt), lambda h, i: (h, 0, i))],
        out_specs=[out_spec] * 4,
        out_shape=[out_sds] * 4,
        compiler_params=_cparams("parallel", "parallel"),
        name="peer_route",
    )(scores)


EXPERT_TM = 512
EXPERT_TE = 512
INV_SQRT2 = 1.0 / math.sqrt(2.0)


def _peer_expert_kernel(x_ref, u_ref, v_ref, n_ref, e0_ref, r1_ref, e1_ref, h_ref, g_ref, b_ref,
                        o_ref, acc_ref):
    e = pl.program_id(1)
    te = u_ref.shape[0]
    rows_per_step = te // PEER_N_KEYS

    @pl.when(e == 0)
    def _():
        acc_ref[...] = jnp.zeros_like(acc_ref)

    a = _dot_nt(x_ref[...], u_ref[...])
    act = 0.5 * a * (1.0 + lax.erf(a * INV_SQRT2))
    gate_cols = []
    for ib in range(rows_per_step):
        i = e * rows_per_step + ib
        w_t = jnp.zeros((PEER_N_KEYS, x_ref.shape[0]), F32)
        for hd in range(PEER_HEADS):
            n_row = n_ref[hd, pl.ds(i, 1), :]
            e0_row = e0_ref[hd, pl.ds(i, 1), :]
            w_t = w_t + jnp.where(r1_ref[hd] < n_row, e1_ref[hd], 0.0) * e0_row
        gate_cols.append(w_t.T)
    gate = jnp.concatenate(gate_cols, axis=1)
    hid = (gate * act).astype(BF16)
    acc_ref[...] += jnp.dot(hid, v_ref[...], preferred_element_type=F32)

    @pl.when(e == pl.num_programs(1) - 1)
    def _():
        y = DEEPNORM_ALPHA * h_ref[...] + acc_ref[...]
        o_ref[...] = _layer_norm(y, g_ref[...], b_ref[...])


def _peer_experts(h16, u16, v16, n_t, e0_t, r1_t, e1_t, h32, g, b):
    t = h16.shape[0]
    tm, te = EXPERT_TM, EXPERT_TE
    assert t % tm == 0 and PEER_N_EXPERTS % te == 0 and te % PEER_N_KEYS == 0
    route = pl.BlockSpec((PEER_HEADS, PEER_N_KEYS, tm), lambda i, e: (0, 0, i))
    row = pl.BlockSpec((1, D_MODEL), lambda i, e: (0, 0))
    return pl.pallas_call(
        _peer_expert_kernel,
        grid=(t // tm, PEER_N_EXPERTS // te),
        in_specs=[
            pl.BlockSpec((tm, D_MODEL), lambda i, e: (i, 0)),
            pl.BlockSpec((te, D_MODEL), lambda i, e: (e, 0)),
            pl.BlockSpec((te, D_MODEL), lambda i, e: (e, 0)),
            route, route, route, route,
            pl.BlockSpec((tm, D_MODEL), lambda i, e: (i, 0)),
            row, row,
        ],
        out_specs=pl.BlockSpec((tm, D_MODEL), lambda i, e: (i, 0)),
        out_shape=jax.ShapeDtypeStruct((t, D_MODEL), F32),
        scratch_shapes=[pltpu.VMEM((tm, D_MODEL), F32)],
        compiler_params=_cparams("parallel", "arbitrary"),
        name="peer_experts",
    )(h16, u16, v16, n_t, e0_t, r1_t, e1_t, h32, g, b)


def kernel(x, ln_emb_g, ln_emb_b, w_in, dil_norm_g, lambda_q1, lambda_k1, lambda_q2, lambda_k2, subln_g,
           w_out, ln1_g, ln1_b, peer_w_query, peer_sub_keys, peer_u, peer_v, ln2_g, ln2_b):
    bsz, seq, d_model = x.shape
    assert d_model == D_MODEL and w_in.shape[0] == DEPTH == 1
    t = bsz * seq
    row = lambda p: p.reshape(1, -1).astype(F32)
    lam_init = 0.8 - 0.6 * math.exp(-0.3 * 0)

    h0, proj = _inproj(x.reshape(t, D_MODEL), row(ln_emb_g), row(ln_emb_b), w_in[0].astype(BF16), seq)
    proj3 = proj.reshape(bsz, seq, IN_WIDTH)

    outs, lses = zip(*[_dil_attn(proj3, window, dilation) for window, dilation in DIL_CONFIGS])
    mix_a = _dil_mix(outs, lses, row(dil_norm_g[0]))
    mix_d = _diff_attn(proj3, row(lambda_q1[0]), row(lambda_k1[0]), row(lambda_q2[0]), row(lambda_k2[0]),
                       row(subln_g[0]), lam_init).reshape(t, DIFF_WIDTH)

    w_out16 = w_out[0].astype(BF16)
    h1, h1_16 = _outproj(mix_a, mix_d, w_out16[:DIL_WIDTH], w_out16[DIL_WIDTH:], h0,
                         row(ln1_g[0]), row(ln1_b[0]))

    keys16 = peer_sub_keys[0].reshape(N_KEYSETS, PEER_N_KEYS, HALF_KEY_DIM).astype(BF16)
    scores = _peer_scores(h1_16, peer_w_query[0].astype(BF16), keys16)
    n_t, e0_t, r1_t, e1_t = _peer_route(scores)
    out = _peer_experts(h1_16, peer_u[0].astype(BF16), peer_v[0].astype(BF16), n_t, e0_t, r1_t, e1_t,
                        h1, row(ln2_g[0]), row(ln2_b[0]))
    return out.reshape(bsz, seq, D_MODEL)
```

```python
import functools
import math

import jax
import jax.numpy as jnp
from jax import lax
from jax.experimental import pallas as pl
from jax.experimental.pallas import tpu as pltpu

F32 = jnp.float32
BF16 = jnp.bfloat16

D_MODEL = 2048
DEPTH = 1
HEAD_DIM = 128
DIFF_WIDTH = D_MODEL // 4
DIL_WIDTH = D_MODEL - DIFF_WIDTH
N_DIL_HEADS = DIL_WIDTH // HEAD_DIM
DIL_CONFIGS = ((128, 1), (512, 4), (2048, 16))
DIFF_QK_DIM = 64
DIFF_V_DIM = 2 * DIFF_QK_DIM
N_DIFF_HEADS = DIFF_WIDTH // DIFF_V_DIM
DIFF_QK_WIDTH = N_DIFF_HEADS * 2 * DIFF_QK_DIM
IN_WIDTH = 3 * DIL_WIDTH + 2 * DIFF_QK_WIDTH + DIFF_WIDTH
ROPE_THETA = 500000.0
ROPE_FRACTION = 4
PEER_HEADS = 8
PEER_N_KEYS = 128
PEER_N_EXPERTS = PEER_N_KEYS * PEER_N_KEYS
PEER_KEY_DIM = 256
PEER_TOPK = 16
LN_EPS = 1e-5
NEG_BIG = -1e30
DEEPNORM_ALPHA = (2.0 * DEPTH) ** 0.25

LANES = 128
BF16_ROWS = 16
VMEM_LIMIT = 56 * 1024 * 1024

COL_QA = 0
COL_KA = DIL_WIDTH // LANES
COL_VA = 2 * DIL_WIDTH // LANES
COL_QD = 3 * DIL_WIDTH // LANES
COL_KD = COL_QD + DIFF_QK_WIDTH // LANES
COL_VD = COL_KD + DIFF_QK_WIDTH // LANES


def _cparams(*sem):
    return pltpu.CompilerParams(dimension_semantics=sem, vmem_limit_bytes=VMEM_LIMIT)


def _layer_norm(x, g, b):
    mu = jnp.mean(x, -1, keepdims=True)
    xc = x - mu
    var = jnp.mean(xc * xc, -1, keepdims=True)
    return xc * lax.rsqrt(var + LN_EPS) * g + b


def _head_rms(t, g):
    return t * lax.rsqrt(jnp.mean(t * t, -1, keepdims=True) + LN_EPS) * g


def _dot_nt(a, b):
    return lax.dot_general(a, b, (((1,), (1,)), ((), ())), preferred_element_type=F32)


INPROJ_TM = 512
INPROJ_TN = 512
ROPE_DIL_SHIFT = HEAD_DIM // ROPE_FRACTION // 2
ROPE_DIFF_SHIFT = DIFF_QK_DIM // ROPE_FRACTION // 2


def _rope_tables(seq, period, half):
    rot = 2 * half
    inv_freq = 1.0 / (ROPE_THETA ** (jnp.arange(0, rot, 2, dtype=F32) / rot))
    ang = jnp.arange(seq, dtype=F32)[:, None] * inv_freq[None, :]
    cos, sin = jnp.cos(ang), jnp.sin(ang)
    pad = period - rot
    c = jnp.concatenate([cos, cos, jnp.ones((seq, pad), F32)], axis=1)
    a = jnp.concatenate([-sin, jnp.zeros((seq, half + pad), F32)], axis=1)
    b = jnp.concatenate([jnp.zeros((seq, half), F32), sin, jnp.zeros((seq, pad), F32)], axis=1)
    reps = LANES // period
    return tuple(jnp.tile(t, (1, reps)) for t in (c, a, b))


ROPE_KIND_DIL, ROPE_KIND_DIFF, ROPE_KIND_NONE = 0, 1, 2


def _inproj_kernel(x_ref, g_ref, b_ref, w_ref, c_ref, a_ref, bt_ref, h_ref, o_ref, xs_ref, *, kind_of_block):
    j = pl.program_id(1)

    @pl.when(j == 0)
    def _():
        h = _layer_norm(x_ref[...], g_ref[...], b_ref[...])
        h_ref[...] = h
        xs_ref[...] = h.astype(BF16)

    acc = jnp.dot(xs_ref[...], w_ref[...], preferred_element_type=F32)
    shift = jnp.where(kind_of_block(j) == ROPE_KIND_DIFF, ROPE_DIFF_SHIFT, ROPE_DIL_SHIFT)
    c, a, b = c_ref[...], a_ref[...], bt_ref[...]
    for gidx in range(acc.shape[1] // LANES):
        t = acc[:, gidx * LANES:(gidx + 1) * LANES]
        r = t * c + pltpu.roll(t, LANES - shift, 1) * a + pltpu.roll(t, shift, 1) * b
        o_ref[:, gidx * LANES:(gidx + 1) * LANES] = r.astype(o_ref.dtype)


def _inproj(x2, g, b, w16, seq):
    t = x2.shape[0]
    tm, tn = INPROJ_TM, INPROJ_TN
    assert t % tm == 0 and seq % tm == 0 and IN_WIDTH % tn == 0
    assert (2 * DIL_WIDTH) % tn == 0 and (3 * DIL_WIDTH) % tn == 0 and DIFF_QK_WIDTH * 2 % tn == 0
    sblocks = seq // tm
    n_dil, diff_lo = 2 * DIL_WIDTH // tn, 3 * DIL_WIDTH // tn
    diff_hi = (3 * DIL_WIDTH + 2 * DIFF_QK_WIDTH) // tn

    def kind_of_block(j):
        return jnp.where(j < n_dil, ROPE_KIND_DIL,
                         jnp.where(jnp.logical_and(j >= diff_lo, j < diff_hi), ROPE_KIND_DIFF, ROPE_KIND_NONE))

    ident = (jnp.ones((seq, LANES), F32), jnp.zeros((seq, LANES), F32), jnp.zeros((seq, LANES), F32))
    tabs = [jnp.stack(per_kind) for per_kind in zip(_rope_tables(seq, HEAD_DIM, ROPE_DIL_SHIFT),
                                                     _rope_tables(seq, DIFF_QK_DIM, ROPE_DIFF_SHIFT), ident)]
    tab_spec = pl.BlockSpec((None, tm, LANES), lambda i, j: (kind_of_block(j), i % sblocks, 0))
    return pl.pallas_call(
        functools.partial(_inproj_kernel, kind_of_block=kind_of_block),
        grid=(t // tm, IN_WIDTH // tn),
        in_specs=[
            pl.BlockSpec((tm, D_MODEL), lambda i, j: (i, 0)),
            pl.BlockSpec((1, D_MODEL), lambda i, j: (0, 0)),
            pl.BlockSpec((1, D_MODEL), lambda i, j: (0, 0)),
            pl.BlockSpec((D_MODEL, tn), lambda i, j: (0, j)),
            tab_spec, tab_spec, tab_spec,
        ],
        out_specs=[
            pl.BlockSpec((tm, D_MODEL), lambda i, j: (i, 0)),
            pl.BlockSpec((tm, tn), lambda i, j: (i, j)),
        ],
        out_shape=[
            jax.ShapeDtypeStruct((t, D_MODEL), F32),
            jax.ShapeDtypeStruct((t, IN_WIDTH), BF16),
        ],
        scratch_shapes=[pltpu.VMEM((tm, D_MODEL), BF16)],
        compiler_params=_cparams("parallel", "arbitrary"),
        name="inproj",
    )(x2, g, b, w16, *tabs)


DIL_BQ = 128
DIL_WIN = 256
DIL_UNROLL = 4


def _dil_attn_kernel(q_ref, k_ref, v_ref, g_ref, o_ref,
                     q32, k32, v32, qd, kd, vd, acc, m_s, l_s, *, seq):
    scale = HEAD_DIM ** -0.5
    q32[...] = q_ref[...].astype(F32)
    k32[...] = k_ref[...].astype(F32)
    v32[...] = v_ref[...].astype(F32)

    def run_config(first, window, d):
        length = seq // d
        half = window // (2 * d)
        bq = min(DIL_BQ, length)
        win = min(DIL_WIN, length)
        assert win >= min(length, bq + 2 * half) and length % bq == 0 and half % BF16_ROWS == 0
        nblk = length // bq
        unroll = min(DIL_UNROLL, nblk)
        assert nblk % unroll == 0

        def residue(r):
            if d == 1:
                qs, ks, vs = q_ref, k_ref, v_ref
            else:
                sub = pl.ds(r, length, stride=d)
                qd[pl.ds(0, length), :] = q32[sub, :].astype(BF16)
                kd[pl.ds(0, length), :] = k32[sub, :].astype(BF16)
                vd[pl.ds(0, length), :] = v32[sub, :].astype(BF16)
                qs, ks, vs = qd, kd, vd

            def block(q0):
                k0 = pl.multiple_of(jnp.clip(q0 - half, 0, length - win), BF16_ROWS)
                s = _dot_nt(qs[pl.ds(q0, bq), :], ks[pl.ds(k0, win), :]) * scale
                qpos = q0 + lax.broadcasted_iota(jnp.int32, (bq, win), 0)
                kpos = k0 + lax.broadcasted_iota(jnp.int32, (bq, win), 1)
                s = jnp.where(jnp.abs(qpos - kpos) <= half, s, NEG_BIG)
                m_blk = jnp.max(s, -1, keepdims=True)
                v = vs[pl.ds(k0, win), :]
                rows = pl.ds(q0, bq) if d == 1 else pl.ds(r + d * q0, bq, stride=d)
                if first:
                    m_new = m_blk
                    p = jnp.exp(s - m_new)
                    l_new = jnp.sum(p, -1, keepdims=True)
                    a_new = jnp.dot(p.astype(BF16), v, preferred_element_type=F32)
                else:
                    m_old = m_s[rows, :]
                    m_new = jnp.maximum(m_old, m_blk)
                    alpha = jnp.exp(m_old - m_new)
                    p = jnp.exp(s - m_new)
                    l_new = alpha * l_s[rows, :] + jnp.sum(p, -1, keepdims=True)
                    a_new = alpha * acc[rows, :] + jnp.dot(p.astype(BF16), v, preferred_element_type=F32)
                acc[rows, :] = a_new
                m_s[rows, :] = m_new
                l_s[rows, :] = l_new

            def body(n, carry):
                for u in range(unroll):
                    block(pl.multiple_of((n * unroll + u) * bq, bq))
                return carry

            lax.fori_loop(0, nblk // unroll, body, 0)

        if d == 1:
            residue(0)
        else:
            def rbody(r, carry):
                residue(r)
                return carry
            lax.fori_loop(0, d, rbody, 0)

    for ci, (window, d) in enumerate(DIL_CONFIGS):
        run_config(ci == 0, window, d)

    o_ref[...] = _head_rms(acc[...] / l_s[...], g_ref[...]).astype(o_ref.dtype)


def _dil_attn(proj3, g):
    bsz, seq, _ = proj3.shape
    dmax = max(d for _, d in DIL_CONFIGS if d > 1)
    dmin = min(d for _, d in DIL_CONFIGS if d > 1)
    assert seq % dmax == 0

    def spec(col0):
        return pl.BlockSpec((None, seq, LANES), lambda b, h: (b, 0, col0 + h))

    return pl.pallas_call(
        functools.partial(_dil_attn_kernel, seq=seq),
        grid=(bsz, N_DIL_HEADS),
        in_specs=[spec(COL_QA), spec(COL_KA), spec(COL_VA), pl.BlockSpec((1, LANES), lambda b, h: (0, 0))],
        out_specs=pl.BlockSpec((None, seq, LANES), lambda b, h: (b, 0, h)),
        out_shape=jax.ShapeDtypeStruct((bsz, seq, DIL_WIDTH), BF16),
        scratch_shapes=[pltpu.VMEM((seq, LANES), F32)] * 3
        + [pltpu.VMEM((seq // dmin, LANES), BF16)] * 3
        + [pltpu.VMEM((seq, LANES), F32), pltpu.VMEM((seq, 1), F32), pltpu.VMEM((seq, 1), F32)],
        compiler_params=_cparams("parallel", "parallel"),
        name="dil_attn",
    )(proj3, proj3, proj3, g)


DIFF_TQ = 256


def _diff_attn_kernel(q_ref, k_ref, v_ref, lq1, lk1, lq2, lk2, g_ref, o_ref, *, lam_init):
    scale = DIFF_QK_DIM ** -0.5
    q = q_ref[...]
    k = k_ref[...]
    v = v_ref[...]
    lo = lax.broadcasted_iota(jnp.int32, q.shape, 1) < DIFF_QK_DIM
    zero = jnp.zeros_like(q)

    def one_map(qm):
        s = _dot_nt(qm, k) * scale
        m = jnp.max(s, -1, keepdims=True)
        p = jnp.exp(s - m)
        den = jnp.sum(p, -1, keepdims=True)
        return jnp.dot(p.astype(BF16), v, preferred_element_type=F32) / den

    o0 = one_map(jnp.where(lo, q, zero))
    o1 = one_map(jnp.where(lo, zero, q))
    lam = (jnp.exp(jnp.sum(lq1[...] * lk1[...], keepdims=True))
           - jnp.exp(jnp.sum(lq2[...] * lk2[...], keepdims=True)) + lam_init)
    a = o0 - lam * o1
    o_ref[...] = (_head_rms(a, g_ref[...]) * (1.0 - lam_init)).astype(o_ref.dtype)


def _diff_attn(proj3, lq1, lk1, lq2, lk2, g, lam_init):
    bsz, seq, _ = proj3.shape
    tq = DIFF_TQ
    assert seq % tq == 0
    vec = pl.BlockSpec((1, DIFF_QK_DIM), lambda b, h, i: (0, 0))
    return pl.pallas_call(
        functools.partial(_diff_attn_kernel, lam_init=lam_init),
        grid=(bsz, N_DIFF_HEADS, seq // tq),
        in_specs=[
            pl.BlockSpec((None, tq, LANES), lambda b, h, i: (b, i, COL_QD + h)),
            pl.BlockSpec((None, seq, LANES), lambda b, h, i: (b, 0, COL_KD + h)),
            pl.BlockSpec((None, seq, LANES), lambda b, h, i: (b, 0, COL_VD + h)),
            vec, vec, vec, vec,
            pl.BlockSpec((1, DIFF_V_DIM), lambda b, h, i: (0, 0)),
        ],
        out_specs=pl.BlockSpec((None, tq, LANES), lambda b, h, i: (b, i, h)),
        out_shape=jax.ShapeDtypeStruct((bsz, seq, DIFF_WIDTH), BF16),
        compiler_params=_cparams("parallel", "parallel", "arbitrary"),
        name="diff_attn",
    )(proj3, proj3, proj3, lq1, lk1, lq2, lk2, g)


OUTPROJ_TM = 256


def _outproj_kernel(ma_ref, md_ref, wa_ref, wd_ref, h_ref, g_ref, b_ref, o32_ref, o16_ref, o16t_ref):
    mix = jnp.dot(ma_ref[...], wa_ref[...], preferred_element_type=F32)
    mix = mix + jnp.dot(md_ref[...], wd_ref[...], preferred_element_type=F32)
    h = _layer_norm(DEEPNORM_ALPHA * h_ref[...] + mix, g_ref[...], b_ref[...])
    o32_ref[...] = h
    o16_ref[...] = h.astype(BF16)
    o16t_ref[...] = h.T.astype(BF16)


def _outproj(mix_a, mix_d, w_a, w_d, h0, g, b):
    t = h0.shape[0]
    tm = OUTPROJ_TM
    assert t % tm == 0
    row = pl.BlockSpec((1, D_MODEL), lambda i: (0, 0))
    return pl.pallas_call(
        _outproj_kernel,
        grid=(t // tm,),
        in_specs=[
            pl.BlockSpec((tm, DIL_WIDTH), lambda i: (i, 0)),
            pl.BlockSpec((tm, DIFF_WIDTH), lambda i: (i, 0)),
            pl.BlockSpec((DIL_WIDTH, D_MODEL), lambda i: (0, 0)),
            pl.BlockSpec((DIFF_WIDTH, D_MODEL), lambda i: (0, 0)),
            pl.BlockSpec((tm, D_MODEL), lambda i: (i, 0)),
            row, row,
        ],
        out_specs=[pl.BlockSpec((tm, D_MODEL), lambda i: (i, 0))] * 2
        + [pl.BlockSpec((D_MODEL, tm), lambda i: (0, i))],
        out_shape=[jax.ShapeDtypeStruct((t, D_MODEL), F32), jax.ShapeDtypeStruct((t, D_MODEL), BF16),
                   jax.ShapeDtypeStruct((D_MODEL, t), BF16)],
        compiler_params=_cparams("parallel"),
        name="outproj",
    )(mix_a, mix_d, w_a, w_d, h0, g, b)


SCORE_TM = 512
N_KEYSETS = 2 * PEER_HEADS
HALF_KEY_DIM = PEER_KEY_DIM // 2


def _peer_score_kernel(h_ref, wq_ref, keys_ref, s_ref):
    q = jnp.dot(h_ref[...], wq_ref[...], preferred_element_type=F32).astype(BF16)
    for hc in range(N_KEYSETS):
        qs = q[:, hc * HALF_KEY_DIM:(hc + 1) * HALF_KEY_DIM]
        s_ref[hc] = _dot_nt(keys_ref[hc], qs)


def _peer_scores(h16, wq16, keys16):
    t = h16.shape[0]
    tm = SCORE_TM
    assert t % tm == 0 and HALF_KEY_DIM == LANES
    return pl.pallas_call(
        _peer_score_kernel,
        grid=(t // tm,),
        in_specs=[
            pl.BlockSpec((tm, D_MODEL), lambda i: (i, 0)),
            pl.BlockSpec((D_MODEL, PEER_HEADS * PEER_KEY_DIM), lambda i: (0, 0)),
            pl.BlockSpec((N_KEYSETS, PEER_N_KEYS, HALF_KEY_DIM), lambda i: (0, 0, 0)),
        ],
        out_specs=pl.BlockSpec((N_KEYSETS, PEER_N_KEYS, tm), lambda i: (0, 0, i)),
        out_shape=jax.ShapeDtypeStruct((N_KEYSETS, PEER_N_KEYS, t), F32),
        compiler_params=_cparams("parallel"),
        name="peer_scores",
    )(h16, wq16, keys16)


ROUTE_TT = 512


def _top_ranks_distinct(v, k):
    rank = jnp.full(v.shape, k, jnp.int32)
    work = v
    vals = []
    for r in range(k):
        m = jnp.max(work, axis=0, keepdims=True)
        sel = work == m
        rank = jnp.where(sel, r, rank)
        work = jnp.where(sel, -jnp.inf, work)
        vals.append(m)
    ranked = jnp.sum((rank < k).astype(F32), axis=0, keepdims=True)
    return jnp.concatenate(vals, axis=0), rank, ranked


def _top_ranks(v, k):
    n = v.shape[0]
    rows = lax.broadcasted_iota(jnp.int32, v.shape, 0)
    rank = jnp.full(v.shape, k, jnp.int32)
    work = v
    vals = []
    for r in range(k):
        m = jnp.max(work, axis=0, keepdims=True)
        first = jnp.min(jnp.where(work == m, rows, n), axis=0, keepdims=True)
        sel = rows == first
        rank = jnp.where(sel, r, rank)
        work = jnp.where(sel, -jnp.inf, work)
        vals.append(m)
    return jnp.concatenate(vals, axis=0), rank


def _staircase(sv0, sv1, k):
    arow = lax.broadcasted_iota(jnp.int32, sv0.shape, 0)
    count = jnp.zeros(sv0.shape, jnp.int32)
    front = sv0 + sv1[0:1]
    best = front[0:1]
    z = jnp.zeros_like(best)
    for _ in range(k):
        m = jnp.max(front, axis=0, keepdims=True)
        first = jnp.min(jnp.where(front == m, arow, k), axis=0, keepdims=True)
        win = arow == first
        z = z + jnp.exp(m - best)
        count = count + win.astype(jnp.int32)
        nw = jnp.sum(jnp.where(win, count, 0), axis=0, keepdims=True)
        nxt = jnp.sum(jnp.where(arow == nw, sv1, 0.0), axis=0, keepdims=True)
        nxt = jnp.where(nw < k, nxt, -jnp.inf)
        front = jnp.where(win, sv0 + nxt, front)
    return count.astype(F32), z


def _peer_route_kernel(s_ref, n_ref, e0_ref, r1_ref, e1_ref, sv0_s, rank0_s, sv1_s, rank1_s, cnt_s, zinv_s):
    k = PEER_TOPK
    tt = s_ref.shape[2]
    sides = ((sv0_s, rank0_s), (sv1_s, rank1_s))

    def lane_col(c):
        return pl.ds(pl.multiple_of(c * LANES, LANES), LANES)

    def rank_col(c, most_ranked):
        for side, (sv_s, rank_s) in enumerate(sides):
            sv, rank, ranked = _top_ranks_distinct(s_ref[side, :, lane_col(c)], k)
            sv_s[:, lane_col(c)] = sv
            rank_s[:, lane_col(c)] = rank
            most_ranked = jnp.maximum(most_ranked, jnp.max(ranked))
        return most_ranked

    most_ranked = lax.fori_loop(0, tt // LANES, rank_col, jnp.zeros((), F32))

    @pl.when(most_ranked > k)
    def _():
        for side, (sv_s, rank_s) in enumerate(sides):
            sv, rank = _top_ranks(s_ref[side], k)
            sv_s[...] = sv
            rank_s[...] = rank

    count, z = _staircase(sv0_s[...], sv1_s[...], k)
    cnt_s[...] = count
    zinv_s[...] = 1.0 / z

    def emit_col(c, carry):
        col = lane_col(c)
        rank0 = rank0_s[:, col]
        cnt = cnt_s[:, col]
        n_of_row = jnp.zeros(rank0.shape, F32)
        for a in range(k):
            n_of_row = jnp.where(rank0 == a, cnt[a:a + 1], n_of_row)
        n_ref[:, col] = n_of_row
        e0_ref[:, col] = jnp.exp(s_ref[0, :, col] - sv0_s[0:1, col])
        r1_ref[:, col] = rank1_s[:, col].astype(F32).astype(r1_ref.dtype)
        e1_ref[:, col] = (jnp.exp(s_ref[1, :, col] - sv1_s[0:1, col]) * zinv_s[:, col]).astype(e1_ref.dtype)
        return carry

    lax.fori_loop(0, tt // LANES, emit_col, 0)


def _peer_route(scores):
    t = scores.shape[-1]
    tt = ROUTE_TT
    assert t % tt == 0 and tt % LANES == 0
    out_spec = pl.BlockSpec((None, PEER_N_KEYS, tt), lambda h, i: (h, 0, i))
    shape = (PEER_HEADS, PEER_N_KEYS, t)
    return pl.pallas_call(
        _peer_route_kernel,
        grid=(PEER_HEADS, t // tt),
        in_specs=[pl.BlockSpec((2, PEER_N_KEYS, tt), lambda h, i: (h, 0, i))],
        out_specs=[out_spec] * 4,
        out_shape=[jax.ShapeDtypeStruct(shape, F32), jax.ShapeDtypeStruct(shape, F32),
                   jax.ShapeDtypeStruct(shape, BF16), jax.ShapeDtypeStruct(shape, BF16)],
        scratch_shapes=[pltpu.VMEM((PEER_TOPK, tt), F32), pltpu.VMEM((PEER_N_KEYS, tt), jnp.int32),
                        pltpu.VMEM((PEER_TOPK, tt), F32), pltpu.VMEM((PEER_N_KEYS, tt), jnp.int32),
                        pltpu.VMEM((PEER_TOPK, tt), F32), pltpu.VMEM((1, tt), F32)],
        compiler_params=_cparams("parallel", "parallel"),
        name="peer_route",
    )(scores)


EXPERT_TM = 512
EXPERT_TE = 256
INV_SQRT2 = 1.0 / math.sqrt(2.0)


def _peer_expert_kernel(xt_ref, ua_ref, ub_ref, vta_ref, vtb_ref, n_ref, e0_ref, r1_ref, e1_ref,
                        h_ref, g_ref, b_ref, o_ref, acc_ref, pre_ref, hid_ref):
    s = pl.program_id(1)
    n_blocks = 2 * (pl.num_programs(1) - 1)
    te = ua_ref.shape[0]
    tm = xt_ref.shape[1]
    rows_per_block = te // PEER_N_KEYS
    zero = jnp.zeros((), BF16)

    @pl.when(s == 0)
    def _():
        acc_ref[...] = jnp.zeros_like(acc_ref)
        pre_ref[...] = jnp.zeros_like(pre_ref)
        hid_ref[...] = jnp.zeros_like(hid_ref)

    def hidden(block, a):
        act = (0.5 * a * (1.0 + lax.erf(a * INV_SQRT2))).astype(BF16)
        first_row = jnp.clip(block, 0, n_blocks - 1) * rows_per_block
        gates = []
        for ib in range(rows_per_block):
            i = first_row + ib
            w = jnp.zeros((PEER_N_KEYS, tm), BF16)
            for hd in range(PEER_HEADS):
                n_b = jnp.broadcast_to(n_ref[hd, pl.ds(i, 1), :], (PEER_N_KEYS, tm)).astype(BF16)
                e0_b = jnp.broadcast_to(e0_ref[hd, pl.ds(i, 1), :], (PEER_N_KEYS, tm)).astype(BF16)
                w = w + jnp.where(r1_ref[hd] < n_b, e1_ref[hd], zero) * e0_b
            gates.append(w)
        return jnp.concatenate(gates, axis=0) * act

    xt = xt_ref[...]
    acc_ref[...] += jnp.dot(vta_ref[...], hid_ref[0], preferred_element_type=F32)
    pre_ref[0] = jnp.dot(ua_ref[...], xt, preferred_element_type=F32)
    hid_ref[1] = hidden(2 * s - 1, pre_ref[1])
    acc_ref[...] += jnp.dot(vtb_ref[...], hid_ref[1], preferred_element_type=F32)
    pre_ref[1] = jnp.dot(ub_ref[...], xt, preferred_element_type=F32)
    hid_ref[0] = hidden(2 * s, pre_ref[0])

    @pl.when(s == pl.num_programs(1) - 1)
    def _():
        y = DEEPNORM_ALPHA * h_ref[...] + acc_ref[...].T
        o_ref[...] = _layer_norm(y, g_ref[...], b_ref[...])


def _peer_experts(h16t, u16, v16t, n_t, e0_t, r1_t, e1_t, h32, g, b):
    t = h16t.shape[1]
    tm, te = EXPERT_TM, EXPERT_TE
    assert t % tm == 0 and PEER_N_EXPERTS % (2 * te) == 0 and te % PEER_N_KEYS == 0
    n_blocks = PEER_N_EXPERTS // te
    last = n_blocks - 1
    route = pl.BlockSpec((PEER_HEADS, PEER_N_KEYS, tm), lambda i, s: (0, 0, i))
    row = pl.BlockSpec((1, D_MODEL), lambda i, s: (0, 0))

    def u_spec(offset):
        return pl.BlockSpec((te, D_MODEL), lambda i, s: (jnp.minimum(2 * s + offset, last), 0))

    def vt_spec(offset):
        return pl.BlockSpec((D_MODEL, te), lambda i, s: (0, jnp.clip(2 * s + offset, 0, last)))

    return pl.pallas_call(
        _peer_expert_kernel,
        grid=(t // tm, n_blocks // 2 + 1),
        in_specs=[
            pl.BlockSpec((D_MODEL, tm), lambda i, s: (0, i)),
            u_spec(0), u_spec(1), vt_spec(-2), vt_spec(-1),
            route, route, route, route,
            pl.BlockSpec((tm, D_MODEL), lambda i, s: (i, 0)),
            row, row,
        ],
        out_specs=pl.BlockSpec((tm, D_MODEL), lambda i, s: (i, 0)),
        out_shape=jax.ShapeDtypeStruct((t, D_MODEL), F32),
        scratch_shapes=[pltpu.VMEM((D_MODEL, tm), F32), pltpu.VMEM((2, te, tm), F32),
                        pltpu.VMEM((2, te, tm), BF16)],
        compiler_params=_cparams("parallel", "arbitrary"),
        name="peer_experts",
    )(h16t, u16, u16, v16t, v16t, n_t, e0_t, r1_t, e1_t, h32, g, b)


def kernel(x, ln_emb_g, ln_emb_b, w_in, dil_norm_g, lambda_q1, lambda_k1, lambda_q2, lambda_k2, subln_g,
           w_out, ln1_g, ln1_b, peer_w_query, peer_sub_keys, peer_u, peer_v, ln2_g, ln2_b):
    bsz, seq, d_model = x.shape
    assert d_model == D_MODEL and w_in.shape[0] == DEPTH == 1
    t = bsz * seq
    row = lambda p: p.reshape(1, -1).astype(F32)
    lam_init = 0.8 - 0.6 * math.exp(-0.3 * 0)

    h0, proj = _inproj(x.reshape(t, D_MODEL), row(ln_emb_g), row(ln_emb_b), w_in[0].astype(BF16), seq)
    proj3 = proj.reshape(bsz, seq, IN_WIDTH)

    mix_a = _dil_attn(proj3, row(dil_norm_g[0])).reshape(t, DIL_WIDTH)
    mix_d = _diff_attn(proj3, row(lambda_q1[0]), row(lambda_k1[0]), row(lambda_q2[0]), row(lambda_k2[0]),
                       row(subln_g[0]), lam_init).reshape(t, DIFF_WIDTH)

    w_out16 = w_out[0].astype(BF16)
    h1, h1_16, h1_16t = _outproj(mix_a, mix_d, w_out16[:DIL_WIDTH], w_out16[DIL_WIDTH:], h0,
                                 row(ln1_g[0]), row(ln1_b[0]))

    keys16 = peer_sub_keys[0].reshape(N_KEYSETS, PEER_N_KEYS, HALF_KEY_DIM).astype(BF16)
    scores = _peer_scores(h1_16, peer_w_query[0].astype(BF16), keys16)
    n_t, e0_t, r1_t, e1_t = _peer_route(scores)
    out = _peer_experts(h1_16t, peer_u[0].astype(BF16), peer_v[0].astype(BF16).T, n_t, e0_t, r1_t, e1_t,
                        h1, row(ln2_g[0]), row(ln2_b[0]))
    return out.reshape(bsz, seq, D_MODEL)
```

```python
import functools
import math

import jax
import jax.numpy as jnp
from jax import lax
from jax.experimental import pallas as pl
from jax.experimental.pallas import tpu as pltpu

F32 = jnp.float32
BF16 = jnp.bfloat16

D_MODEL = 2048
DEPTH = 1
HEAD_DIM = 128
DIFF_WIDTH = D_MODEL // 4
DIL_WIDTH = D_MODEL - DIFF_WIDTH
N_DIL_HEADS = DIL_WIDTH // HEAD_DIM
DIL_CONFIGS = ((128, 1), (512, 4), (2048, 16))
DIFF_QK_DIM = 64
DIFF_V_DIM = 2 * DIFF_QK_DIM
N_DIFF_HEADS = DIFF_WIDTH // DIFF_V_DIM
DIFF_QK_WIDTH = N_DIFF_HEADS * 2 * DIFF_QK_DIM
IN_WIDTH = 3 * DIL_WIDTH + 2 * DIFF_QK_WIDTH + DIFF_WIDTH
ROPE_THETA = 500000.0
ROPE_FRACTION = 4
PEER_HEADS = 8
PEER_N_KEYS = 128
PEER_N_EXPERTS = PEER_N_KEYS * PEER_N_KEYS
PEER_KEY_DIM = 256
PEER_TOPK = 16
LN_EPS = 1e-5
NEG_BIG = -1e30
DEEPNORM_ALPHA = (2.0 * DEPTH) ** 0.25

LANES = 128
BF16_ROWS = 16
VMEM_LIMIT = 56 * 1024 * 1024

COL_QA = 0
COL_KA = DIL_WIDTH // LANES
COL_VA = 2 * DIL_WIDTH // LANES
COL_QD = 3 * DIL_WIDTH // LANES
COL_KD = COL_QD + DIFF_QK_WIDTH // LANES
COL_VD = COL_KD + DIFF_QK_WIDTH // LANES


def _cparams(*sem):
    return pltpu.CompilerParams(dimension_semantics=sem, vmem_limit_bytes=VMEM_LIMIT)


def _layer_norm(x, g, b):
    mu = jnp.mean(x, -1, keepdims=True)
    xc = x - mu
    var = jnp.mean(xc * xc, -1, keepdims=True)
    return xc * lax.rsqrt(var + LN_EPS) * g + b


def _head_rms(t, g):
    return t * lax.rsqrt(jnp.mean(t * t, -1, keepdims=True) + LN_EPS) * g


def _dot_nt(a, b):
    return lax.dot_general(a, b, (((1,), (1,)), ((), ())), preferred_element_type=F32)


INPROJ_TM = 512
INPROJ_TN = 512
ROPE_DIL_SHIFT = HEAD_DIM // ROPE_FRACTION // 2
ROPE_DIFF_SHIFT = DIFF_QK_DIM // ROPE_FRACTION // 2


def _rope_tables(seq, period, half):
    rot = 2 * half
    inv_freq = 1.0 / (ROPE_THETA ** (jnp.arange(0, rot, 2, dtype=F32) / rot))
    ang = jnp.arange(seq, dtype=F32)[:, None] * inv_freq[None, :]
    cos, sin = jnp.cos(ang), jnp.sin(ang)
    pad = period - rot
    c = jnp.concatenate([cos, cos, jnp.ones((seq, pad), F32)], axis=1)
    a = jnp.concatenate([-sin, jnp.zeros((seq, half + pad), F32)], axis=1)
    b = jnp.concatenate([jnp.zeros((seq, half), F32), sin, jnp.zeros((seq, pad), F32)], axis=1)
    reps = LANES // period
    return tuple(jnp.tile(t, (1, reps)) for t in (c, a, b))


ROPE_KIND_DIL, ROPE_KIND_DIFF, ROPE_KIND_NONE = 0, 1, 2


def _inproj_kernel(x_ref, g_ref, b_ref, w_ref, c_ref, a_ref, bt_ref, h_ref, o_ref, xs_ref, raw_ref,
                   *, n_col_blocks, kind_of_block):
    s = pl.program_id(0)
    last_tile = pl.num_programs(0) - 2
    j = jnp.minimum(s, last_tile) % n_col_blocks
    j_prev = jnp.maximum(s - 1, 0) % n_col_blocks

    @pl.when(s == 0)
    def _():
        raw_ref[...] = jnp.zeros_like(raw_ref)

    @pl.when(jnp.logical_and(j == 0, s <= last_tile))
    def _():
        h = _layer_norm(x_ref[...], g_ref[...], b_ref[...])
        h_ref[...] = h
        xs_ref[...] = h.astype(BF16)

    shift = jnp.where(kind_of_block(j_prev) == ROPE_KIND_DIFF, ROPE_DIFF_SHIFT, ROPE_DIL_SHIFT)
    c, a, b = c_ref[...], a_ref[...], bt_ref[...]
    for gidx in range(raw_ref.shape[1] // LANES):
        t = raw_ref[:, gidx * LANES:(gidx + 1) * LANES]
        r = t * c + pltpu.roll(t, LANES - shift, 1) * a + pltpu.roll(t, shift, 1) * b
        o_ref[:, gidx * LANES:(gidx + 1) * LANES] = r.astype(o_ref.dtype)

    raw_ref[...] = jnp.dot(xs_ref[...], w_ref[...], preferred_element_type=F32)


def _inproj(x2, g, b, w16, seq):
    t = x2.shape[0]
    tm, tn = INPROJ_TM, INPROJ_TN
    assert t % tm == 0 and seq % tm == 0 and IN_WIDTH % tn == 0
    assert (2 * DIL_WIDTH) % tn == 0 and (3 * DIL_WIDTH) % tn == 0 and DIFF_QK_WIDTH * 2 % tn == 0
    sblocks = seq // tm
    n_col = IN_WIDTH // tn
    n_tiles = (t // tm) * n_col
    n_dil, diff_lo = 2 * DIL_WIDTH // tn, 3 * DIL_WIDTH // tn
    diff_hi = (3 * DIL_WIDTH + 2 * DIFF_QK_WIDTH) // tn

    def kind_of_block(j):
        return jnp.where(j < n_dil, ROPE_KIND_DIL,
                         jnp.where(jnp.logical_and(j >= diff_lo, j < diff_hi), ROPE_KIND_DIFF, ROPE_KIND_NONE))

    def tile(s):
        s = jnp.minimum(s, n_tiles - 1)
        return s // n_col, s % n_col

    def prev_tile(s):
        return tile(jnp.maximum(s - 1, 0))

    ident = (jnp.ones((seq, LANES), F32), jnp.zeros((seq, LANES), F32), jnp.zeros((seq, LANES), F32))
    tabs = [jnp.stack(per_kind) for per_kind in zip(_rope_tables(seq, HEAD_DIM, ROPE_DIL_SHIFT),
                                                     _rope_tables(seq, DIFF_QK_DIM, ROPE_DIFF_SHIFT), ident)]
    tab_spec = pl.BlockSpec(
        (None, tm, LANES), lambda s: (kind_of_block(prev_tile(s)[1]), prev_tile(s)[0] % sblocks, 0))
    return pl.pallas_call(
        functools.partial(_inproj_kernel, n_col_blocks=n_col, kind_of_block=kind_of_block),
        grid=(n_tiles + 1,),
        in_specs=[
            pl.BlockSpec((tm, D_MODEL), lambda s: (tile(s)[0], 0)),
            pl.BlockSpec((1, D_MODEL), lambda s: (0, 0)),
            pl.BlockSpec((1, D_MODEL), lambda s: (0, 0)),
            pl.BlockSpec((D_MODEL, tn), lambda s: (0, tile(s)[1])),
            tab_spec, tab_spec, tab_spec,
        ],
        out_specs=[
            pl.BlockSpec((tm, D_MODEL), lambda s: (tile(s)[0], 0)),
            pl.BlockSpec((tm, tn), lambda s: prev_tile(s)),
        ],
        out_shape=[
            jax.ShapeDtypeStruct((t, D_MODEL), F32),
            jax.ShapeDtypeStruct((t, IN_WIDTH), BF16),
        ],
        scratch_shapes=[pltpu.VMEM((tm, D_MODEL), BF16), pltpu.VMEM((tm, tn), F32)],
        compiler_params=_cparams("arbitrary"),
        name="inproj",
    )(x2, g, b, w16, *tabs)


DIL_BQ = 128
DIL_WIN = 256
DIL_GROUP = 8


def _dil_plan(seq, window, d):
    length = seq // d
    half = window // (2 * d)
    bq = min(DIL_BQ, length)
    win = min(DIL_WIN, length)
    nblk = length // bq
    blocks = min(nblk, DIL_GROUP)
    residues = max(1, min(d, DIL_GROUP // nblk))
    assert seq % d == 0 and length % bq == 0 and nblk % blocks == 0 and d % residues == 0
    assert win >= min(length, bq + 2 * half) and half % BF16_ROWS == 0
    return length, half, bq, win, nblk, blocks, residues


def _dil_attn_kernel(q_ref, k_ref, v_ref, g_ref, o_ref,
                     q32, k32, v32, qd, kd, vd, vaug, acc, m_s, l_s, *, seq):
    scale = HEAD_DIM ** -0.5
    q32[...] = q_ref[...].astype(F32)
    k32[...] = k_ref[...].astype(F32)
    v32[...] = v_ref[...].astype(F32)
    vaug[:, :LANES] = v_ref[...]
    vaug[:, LANES:] = jnp.ones((seq, LANES), BF16)
    vd[:, LANES:] = jnp.ones((vd.shape[0], LANES), BF16)

    def run_config(first, window, d):
        length, half, bq, win, nblk, blocks, residues = _dil_plan(seq, window, d)

        def group(rg, bg):
            if d == 1:
                qs, ks, vs = q_ref, k_ref, vaug
            else:
                qs, ks, vs = qd, kd, vd
            items = []
            for j in range(residues):
                r = rg * residues + j
                for u in range(blocks):
                    q0 = pl.multiple_of((bg * blocks + u) * bq, bq)
                    k0 = pl.multiple_of(jnp.clip(q0 - half, 0, length - win), BF16_ROWS)
                    rows = pl.ds(q0, bq) if d == 1 else pl.ds(r + d * q0, bq, stride=d)
                    items.append((j * length, q0, k0, rows))
            scores = []
            for base, q0, k0, _ in items:
                s = _dot_nt(qs[pl.ds(base + q0, bq), :], ks[pl.ds(base + k0, win), :]) * scale
                qpos = q0 + lax.broadcasted_iota(jnp.int32, (bq, win), 0)
                kpos = k0 + lax.broadcasted_iota(jnp.int32, (bq, win), 1)
                scores.append(jnp.where(jnp.abs(qpos - kpos) <= half, s, NEG_BIG))
            m_new = [jnp.broadcast_to(jnp.max(s, -1, keepdims=True), (bq, LANES)) for s in scores]
            if not first:
                m_old = [m_s[rows, :] for _, _, _, rows in items]
                m_new = [jnp.maximum(a, b) for a, b in zip(m_old, m_new)]
            def widen(m):
                return jnp.tile(m, (1, win // LANES)) if win % LANES == 0 else m[:, :1]
            probs = [jnp.exp(s - widen(m)).astype(BF16) for s, m in zip(scores, m_new)]
            pv = [jnp.dot(p, vs[pl.ds(base + k0, win), :], preferred_element_type=F32)
                  for p, (base, _, k0, _) in zip(probs, items)]
            for idx, (_, _, _, rows) in enumerate(items):
                a_new, l_new = pv[idx][:, :LANES], pv[idx][:, LANES:]
                if not first:
                    alpha = jnp.exp(m_old[idx] - m_new[idx])
                    a_new = alpha * acc[rows, :] + a_new
                    l_new = alpha * l_s[rows, :] + l_new
                acc[rows, :] = a_new
                m_s[rows, :] = m_new[idx]
                l_s[rows, :] = l_new

        def residue_group(rg, carry):
            if d > 1:
                for j in range(residues):
                    sub = pl.ds(rg * residues + j, length, stride=d)
                    dst = pl.ds(j * length, length)
                    qd[dst, :] = q32[sub, :].astype(BF16)
                    kd[dst, :] = k32[sub, :].astype(BF16)
                    vd[dst, :LANES] = v32[sub, :].astype(BF16)
            if nblk == blocks:
                group(rg, 0)
            else:
                lax.fori_loop(0, nblk // blocks, lambda bg, c: (group(rg, bg), c)[1], 0)
            return carry

        if d == residues:
            residue_group(0, 0)
        else:
            lax.fori_loop(0, d // residues, residue_group, 0)

    for ci, (window, d) in enumerate(DIL_CONFIGS):
        run_config(ci == 0, window, d)

    o_ref[...] = _head_rms(acc[...] / l_s[...], g_ref[...]).astype(o_ref.dtype)


def _dil_attn(proj3, g):
    bsz, seq, _ = proj3.shape
    assert DIL_CONFIGS[0][1] == 1
    plans = [_dil_plan(seq, window, d) for window, d in DIL_CONFIGS if d > 1]
    sub_rows = max(length * residues for length, _, _, _, _, _, residues in plans)

    def spec(col0):
        return pl.BlockSpec((None, seq, LANES), lambda b, h: (b, 0, col0 + h))

    return pl.pallas_call(
        functools.partial(_dil_attn_kernel, seq=seq),
        grid=(bsz, N_DIL_HEADS),
        in_specs=[spec(COL_QA), spec(COL_KA), spec(COL_VA), pl.BlockSpec((1, LANES), lambda b, h: (0, 0))],
        out_specs=pl.BlockSpec((None, seq, LANES), lambda b, h: (b, 0, h)),
        out_shape=jax.ShapeDtypeStruct((bsz, seq, DIL_WIDTH), BF16),
        scratch_shapes=[pltpu.VMEM((seq, LANES), F32)] * 3
        + [pltpu.VMEM((sub_rows, LANES), BF16)] * 2
        + [pltpu.VMEM((sub_rows, 2 * LANES), BF16), pltpu.VMEM((seq, 2 * LANES), BF16)]
        + [pltpu.VMEM((seq, LANES), F32)] * 3,
        compiler_params=_cparams("parallel", "parallel"),
        name="dil_attn",
    )(proj3, proj3, proj3, g)


DIFF_TQ = 256
DIFF_KEY_CHUNK = 1024


def _diff_attn_kernel(q_ref, k_ref, v_ref, lq1, lk1, lq2, lk2, g_ref, o_ref, vaug, s_scr, mx_scr, *, lam_init):
    seq = k_ref.shape[0]
    kc = min(DIFF_KEY_CHUNK, seq)
    assert seq % kc == 0
    chunks = [pl.ds(c * kc, kc) for c in range(seq // kc)]

    @pl.when(pl.program_id(2) == 0)
    def _():
        vaug[:, :LANES] = v_ref[...]
        vaug[:, LANES:] = jnp.ones((seq, LANES), BF16)
        s_scr[...] = jnp.zeros_like(s_scr)
        mx_scr[...] = jnp.zeros_like(mx_scr)

    scale = DIFF_QK_DIM ** -0.5
    assert 2.0 ** round(math.log2(scale)) == scale
    q = q_ref[...] * scale
    lo = lax.broadcasted_iota(jnp.int32, q.shape, 1) < DIFF_QK_DIM
    zero = jnp.zeros_like(q)
    q_maps = (jnp.where(lo, q, zero), jnp.where(lo, zero, q))

    outs = []
    for mp in range(2):
        m_prev = mx_scr[mp][:, :1]
        pv, m_next = None, None
        for c in chunks:
            p = jnp.exp(s_scr[mp, :, c] - m_prev).astype(BF16)
            part = jnp.dot(p, vaug[c, :], preferred_element_type=F32)
            pv = part if pv is None else pv + part
            s_new = _dot_nt(q_maps[mp], k_ref[c, :])
            s_scr[mp, :, c] = s_new
            m_chunk = jnp.max(s_new, -1, keepdims=True)
            m_next = m_chunk if m_next is None else jnp.maximum(m_next, m_chunk)
        mx_scr[mp] = jnp.broadcast_to(m_next, mx_scr.shape[1:])
        outs.append(pv[:, :LANES] / pv[:, LANES:])
    o0, o1 = outs
    lam = (jnp.exp(jnp.sum(lq1[...] * lk1[...], keepdims=True))
           - jnp.exp(jnp.sum(lq2[...] * lk2[...], keepdims=True)) + lam_init)
    a = o0 - lam * o1
    o_ref[...] = (_head_rms(a, g_ref[...]) * (1.0 - lam_init)).astype(o_ref.dtype)


def _diff_attn(proj3, lq1, lk1, lq2, lk2, g, lam_init):
    bsz, seq, _ = proj3.shape
    tq = DIFF_TQ
    assert seq % tq == 0
    n_tiles = seq // tq
    vec = pl.BlockSpec((1, DIFF_QK_DIM), lambda b, h, i: (0, 0))
    return pl.pallas_call(
        functools.partial(_diff_attn_kernel, lam_init=lam_init),
        grid=(bsz, N_DIFF_HEADS, n_tiles + 1),
        in_specs=[
            pl.BlockSpec((None, tq, LANES), lambda b, h, i: (b, jnp.minimum(i, n_tiles - 1), COL_QD + h)),
            pl.BlockSpec((None, seq, LANES), lambda b, h, i: (b, 0, COL_KD + h)),
            pl.BlockSpec((None, seq, LANES), lambda b, h, i: (b, 0, COL_VD + h)),
            vec, vec, vec, vec,
            pl.BlockSpec((1, DIFF_V_DIM), lambda b, h, i: (0, 0)),
        ],
        out_specs=pl.BlockSpec((None, tq, LANES), lambda b, h, i: (b, jnp.maximum(i - 1, 0), h)),
        out_shape=jax.ShapeDtypeStruct((bsz, seq, DIFF_WIDTH), BF16),
        scratch_shapes=[pltpu.VMEM((seq, 2 * LANES), BF16), pltpu.VMEM((2, tq, seq), F32),
                        pltpu.VMEM((2, tq, LANES), F32)],
        compiler_params=_cparams("parallel", "parallel", "arbitrary"),
        name="diff_attn",
    )(proj3, proj3, proj3, lq1, lk1, lq2, lk2, g)


OUTPROJ_TM = 256


def _outproj_kernel(ma_ref, md_ref, wa_ref, wd_ref, h_ref, g_ref, b_ref, o32_ref, o16_ref, o16t_ref):
    mix = jnp.dot(ma_ref[...], wa_ref[...], preferred_element_type=F32)
    mix = mix + jnp.dot(md_ref[...], wd_ref[...], preferred_element_type=F32)
    h = _layer_norm(DEEPNORM_ALPHA * h_ref[...] + mix, g_ref[...], b_ref[...])
    o32_ref[...] = h
    o16_ref[...] = h.astype(BF16)
    o16t_ref[...] = h.T.astype(BF16)


def _outproj(mix_a, mix_d, w_a, w_d, h0, g, b):
    t = h0.shape[0]
    tm = OUTPROJ_TM
    assert t % tm == 0
    row = pl.BlockSpec((1, D_MODEL), lambda i: (0, 0))
    return pl.pallas_call(
        _outproj_kernel,
        grid=(t // tm,),
        in_specs=[
            pl.BlockSpec((tm, DIL_WIDTH), lambda i: (i, 0)),
            pl.BlockSpec((tm, DIFF_WIDTH), lambda i: (i, 0)),
            pl.BlockSpec((DIL_WIDTH, D_MODEL), lambda i: (0, 0)),
            pl.BlockSpec((DIFF_WIDTH, D_MODEL), lambda i: (0, 0)),
            pl.BlockSpec((tm, D_MODEL), lambda i: (i, 0)),
            row, row,
        ],
        out_specs=[pl.BlockSpec((tm, D_MODEL), lambda i: (i, 0))] * 2
        + [pl.BlockSpec((D_MODEL, tm), lambda i: (0, i))],
        out_shape=[jax.ShapeDtypeStruct((t, D_MODEL), F32), jax.ShapeDtypeStruct((t, D_MODEL), BF16),
                   jax.ShapeDtypeStruct((D_MODEL, t), BF16)],
        compiler_params=_cparams("parallel"),
        name="outproj",
    )(mix_a, mix_d, w_a, w_d, h0, g, b)


SCORE_TM = 512
N_KEYSETS = 2 * PEER_HEADS
HALF_KEY_DIM = PEER_KEY_DIM // 2


def _peer_score_kernel(h_ref, wq_ref, keys_ref, s_ref):
    q = jnp.dot(h_ref[...], wq_ref[...], preferred_element_type=F32).astype(BF16)
    for hc in range(N_KEYSETS):
        qs = q[:, hc * HALF_KEY_DIM:(hc + 1) * HALF_KEY_DIM]
        s_ref[hc] = _dot_nt(keys_ref[hc], qs)


def _peer_scores(h16, wq16, keys16):
    t = h16.shape[0]
    tm = SCORE_TM
    assert t % tm == 0 and HALF_KEY_DIM == LANES
    return pl.pallas_call(
        _peer_score_kernel,
        grid=(t // tm,),
        in_specs=[
            pl.BlockSpec((tm, D_MODEL), lambda i: (i, 0)),
            pl.BlockSpec((D_MODEL, PEER_HEADS * PEER_KEY_DIM), lambda i: (0, 0)),
            pl.BlockSpec((N_KEYSETS, PEER_N_KEYS, HALF_KEY_DIM), lambda i: (0, 0, 0)),
        ],
        out_specs=pl.BlockSpec((N_KEYSETS, PEER_N_KEYS, tm), lambda i: (0, 0, i)),
        out_shape=jax.ShapeDtypeStruct((N_KEYSETS, PEER_N_KEYS, t), F32),
        compiler_params=_cparams("parallel"),
        name="peer_scores",
    )(h16, wq16, keys16)


ROUTE_TT = 512


def _top_ranks_distinct(v, k):
    rank = jnp.full(v.shape, k, jnp.int32)
    work = v
    vals = []
    for r in range(k):
        m = jnp.max(work, axis=0, keepdims=True)
        sel = work == m
        rank = jnp.where(sel, r, rank)
        work = jnp.where(sel, -jnp.inf, work)
        vals.append(m)
    ranked = jnp.sum((rank < k).astype(F32), axis=0, keepdims=True)
    return jnp.concatenate(vals, axis=0), rank, ranked


def _top_ranks(v, k):
    n = v.shape[0]
    rows = lax.broadcasted_iota(jnp.int32, v.shape, 0)
    rank = jnp.full(v.shape, k, jnp.int32)
    work = v
    vals = []
    for r in range(k):
        m = jnp.max(work, axis=0, keepdims=True)
        first = jnp.min(jnp.where(work == m, rows, n), axis=0, keepdims=True)
        sel = rows == first
        rank = jnp.where(sel, r, rank)
        work = jnp.where(sel, -jnp.inf, work)
        vals.append(m)
    return jnp.concatenate(vals, axis=0), rank


def _staircase(sv0, sv1, k):
    arow = lax.broadcasted_iota(jnp.int32, sv0.shape, 0)
    count = jnp.zeros(sv0.shape, jnp.int32)
    front = sv0 + sv1[0:1]
    best = front[0:1]
    z = jnp.zeros_like(best)
    for _ in range(k):
        m = jnp.max(front, axis=0, keepdims=True)
        first = jnp.min(jnp.where(front == m, arow, k), axis=0, keepdims=True)
        win = arow == first
        z = z + jnp.exp(m - best)
        count = count + win.astype(jnp.int32)
        nw = jnp.sum(jnp.where(win, count, 0), axis=0, keepdims=True)
        nxt = jnp.sum(jnp.where(arow == nw, sv1, 0.0), axis=0, keepdims=True)
        nxt = jnp.where(nw < k, nxt, -jnp.inf)
        front = jnp.where(win, sv0 + nxt, front)
    return count.astype(F32), z


def _peer_route_kernel(s_ref, n_ref, e0_ref, r1_ref, e1_ref, sv0_s, rank0_s, sv1_s, rank1_s, cnt_s, zinv_s):
    k = PEER_TOPK
    tt = s_ref.shape[2]
    sides = ((sv0_s, rank0_s), (sv1_s, rank1_s))

    def lane_col(c):
        return pl.ds(pl.multiple_of(c * LANES, LANES), LANES)

    def rank_col(c, most_ranked):
        for side, (sv_s, rank_s) in enumerate(sides):
            sv, rank, ranked = _top_ranks_distinct(s_ref[side, :, lane_col(c)], k)
            sv_s[:, lane_col(c)] = sv
            rank_s[:, lane_col(c)] = rank
            most_ranked = jnp.maximum(most_ranked, jnp.max(ranked))
        return most_ranked

    most_ranked = lax.fori_loop(0, tt // LANES, rank_col, jnp.zeros((), F32))

    @pl.when(most_ranked > k)
    def _():
        for side, (sv_s, rank_s) in enumerate(sides):
            sv, rank = _top_ranks(s_ref[side], k)
            sv_s[...] = sv
            rank_s[...] = rank

    count, z = _staircase(sv0_s[...], sv1_s[...], k)
    cnt_s[...] = count
    zinv_s[...] = 1.0 / z

    def emit_col(c, carry):
        col = lane_col(c)
        rank0 = rank0_s[:, col]
        cnt = cnt_s[:, col]
        n_of_row = jnp.zeros(rank0.shape, F32)
        for a in range(k):
            n_of_row = jnp.where(rank0 == a, cnt[a:a + 1], n_of_row)
        n_ref[:, col] = n_of_row
        e0_ref[:, col] = jnp.exp(s_ref[0, :, col] - sv0_s[0:1, col])
        r1_ref[:, col] = rank1_s[:, col].astype(F32).astype(r1_ref.dtype)
        e1_ref[:, col] = (jnp.exp(s_ref[1, :, col] - sv1_s[0:1, col]) * zinv_s[:, col]).astype(e1_ref.dtype)
        return carry

    lax.fori_loop(0, tt // LANES, emit_col, 0)


def _peer_route(scores):
    t = scores.shape[-1]
    tt = ROUTE_TT
    assert t % tt == 0 and tt % LANES == 0
    out_spec = pl.BlockSpec((None, PEER_N_KEYS, tt), lambda h, i: (h, 0, i))
    shape = (PEER_HEADS, PEER_N_KEYS, t)
    return pl.pallas_call(
        _peer_route_kernel,
        grid=(PEER_HEADS, t // tt),
        in_specs=[pl.BlockSpec((2, PEER_N_KEYS, tt), lambda h, i: (h, 0, i))],
        out_specs=[out_spec] * 4,
        out_shape=[jax.ShapeDtypeStruct(shape, F32), jax.ShapeDtypeStruct(shape, F32),
                   jax.ShapeDtypeStruct(shape, BF16), jax.ShapeDtypeStruct(shape, BF16)],
        scratch_shapes=[pltpu.VMEM((PEER_TOPK, tt), F32), pltpu.VMEM((PEER_N_KEYS, tt), jnp.int32),
                        pltpu.VMEM((PEER_TOPK, tt), F32), pltpu.VMEM((PEER_N_KEYS, tt), jnp.int32),
                        pltpu.VMEM((PEER_TOPK, tt), F32), pltpu.VMEM((1, tt), F32)],
        compiler_params=_cparams("parallel", "parallel"),
        name="peer_route",
    )(scores)


EXPERT_TM = 512
EXPERT_TE = 512
EXPERT_DRAIN_STEPS = 2
INV_SQRT2 = 1.0 / math.sqrt(2.0)


def _peer_expert_kernel(xt_ref, u_ref, vt_ref, n_ref, e0_ref, r1_ref, e1_ref, h_ref, g_ref, b_ref,
                        o_ref, acc_ref, pre_ref, hid_ref):
    s = pl.program_id(1)
    n_blocks = pl.num_programs(1) - EXPERT_DRAIN_STEPS
    te = u_ref.shape[0]
    tm = xt_ref.shape[1]
    rows_per_block = te // PEER_N_KEYS
    zero = jnp.zeros((), BF16)

    @pl.when(s == 0)
    def _():
        acc_ref[...] = jnp.zeros_like(acc_ref)
        pre_ref[...] = jnp.zeros_like(pre_ref)
        hid_ref[...] = jnp.zeros_like(hid_ref)

    acc_ref[...] += jnp.dot(vt_ref[...], hid_ref[...], preferred_element_type=F32)

    a = pre_ref[...]
    act = (0.5 * a * (1.0 + lax.erf(a * INV_SQRT2))).astype(BF16)
    first_row = jnp.clip(s - 1, 0, n_blocks - 1) * rows_per_block
    gates = []
    for ib in range(rows_per_block):
        i = first_row + ib
        w = jnp.zeros((PEER_N_KEYS, tm), BF16)
        for hd in range(PEER_HEADS):
            n_b = jnp.broadcast_to(n_ref[hd, pl.ds(i, 1), :], (PEER_N_KEYS, tm)).astype(BF16)
            e0_b = jnp.broadcast_to(e0_ref[hd, pl.ds(i, 1), :], (PEER_N_KEYS, tm)).astype(BF16)
            w = w + jnp.where(r1_ref[hd] < n_b, e1_ref[hd], zero) * e0_b
        gates.append(w)
    hid_ref[...] = jnp.concatenate(gates, axis=0) * act

    pre_ref[...] = jnp.dot(u_ref[...], xt_ref[...], preferred_element_type=F32)

    @pl.when(s == pl.num_programs(1) - 1)
    def _():
        y = DEEPNORM_ALPHA * h_ref[...] + acc_ref[...].T
        o_ref[...] = _layer_norm(y, g_ref[...], b_ref[...])


def _peer_experts(h16t, u16, v16t, n_t, e0_t, r1_t, e1_t, h32, g, b):
    t = h16t.shape[1]
    tm, te = EXPERT_TM, EXPERT_TE
    assert t % tm == 0 and PEER_N_EXPERTS % te == 0 and te % PEER_N_KEYS == 0
    n_blocks = PEER_N_EXPERTS // te
    last = n_blocks - 1
    route = pl.BlockSpec((PEER_HEADS, PEER_N_KEYS, tm), lambda i, s: (0, 0, i))
    row = pl.BlockSpec((1, D_MODEL), lambda i, s: (0, 0))
    return pl.pallas_call(
        _peer_expert_kernel,
        grid=(t // tm, n_blocks + EXPERT_DRAIN_STEPS),
        in_specs=[
            pl.BlockSpec((D_MODEL, tm), lambda i, s: (0, i)),
            pl.BlockSpec((te, D_MODEL), lambda i, s: (jnp.minimum(s, last), 0)),
            pl.BlockSpec((D_MODEL, te), lambda i, s: (0, jnp.clip(s - 2, 0, last))),
            route, route, route, route,
            pl.BlockSpec((tm, D_MODEL), lambda i, s: (i, 0)),
            row, row,
        ],
        out_specs=pl.BlockSpec((tm, D_MODEL), lambda i, s: (i, 0)),
        out_shape=jax.ShapeDtypeStruct((t, D_MODEL), F32),
        scratch_shapes=[pltpu.VMEM((D_MODEL, tm), F32), pltpu.VMEM((te, tm), F32), pltpu.VMEM((te, tm), BF16)],
        compiler_params=_cparams("parallel", "arbitrary"),
        name="peer_experts",
    )(h16t, u16, v16t, n_t, e0_t, r1_t, e1_t, h32, g, b)


def kernel(x, ln_emb_g, ln_emb_b, w_in, dil_norm_g, lambda_q1, lambda_k1, lambda_q2, lambda_k2, subln_g,
           w_out, ln1_g, ln1_b, peer_w_query, peer_sub_keys, peer_u, peer_v, ln2_g, ln2_b):
    bsz, seq, d_model = x.shape
    assert d_model == D_MODEL and w_in.shape[0] == DEPTH == 1
    t = bsz * seq
    row = lambda p: p.reshape(1, -1).astype(F32)
    lam_init = 0.8 - 0.6 * math.exp(-0.3 * 0)

    h0, proj = _inproj(x.reshape(t, D_MODEL), row(ln_emb_g), row(ln_emb_b), w_in[0].astype(BF16), seq)
    proj3 = proj.reshape(bsz, seq, IN_WIDTH)

    mix_a = _dil_attn(proj3, row(dil_norm_g[0])).reshape(t, DIL_WIDTH)
    mix_d = _diff_attn(proj3, row(lambda_q1[0]), row(lambda_k1[0]), row(lambda_q2[0]), row(lambda_k2[0]),
                       row(subln_g[0]), lam_init).reshape(t, DIFF_WIDTH)

    w_out16 = w_out[0].astype(BF16)
    h1, h1_16, h1_16t = _outproj(mix_a, mix_d, w_out16[:DIL_WIDTH], w_out16[DIL_WIDTH:], h0,
                                 row(ln1_g[0]), row(ln1_b[0]))

    keys16 = peer_sub_keys[0].reshape(N_KEYSETS, PEER_N_KEYS, HALF_KEY_DIM).astype(BF16)
    scores = _peer_scores(h1_16, peer_w_query[0].astype(BF16), keys16)
    n_t, e0_t, r1_t, e1_t = _peer_route(scores)
    out = _peer_experts(h1_16t, peer_u[0].astype(BF16), peer_v[0].astype(BF16).T, n_t, e0_t, r1_t, e1_t,
                        h1, row(ln2_g[0]), row(ln2_b[0]))
    return out.reshape(bsz, seq, D_MODEL)
```

```python
import functools
import math

import jax
import jax.numpy as jnp
from jax import lax
from jax.experimental import pallas as pl
from jax.experimental.pallas import tpu as pltpu

F32 = jnp.float32
BF16 = jnp.bfloat16

D_MODEL = 2048
DEPTH = 1
HEAD_DIM = 128
DIFF_WIDTH = D_MODEL // 4
DIL_WIDTH = D_MODEL - DIFF_WIDTH
N_DIL_HEADS = DIL_WIDTH // HEAD_DIM
DIL_CONFIGS = ((128, 1), (512, 4), (2048, 16))
DIFF_QK_DIM = 64
DIFF_V_DIM = 2 * DIFF_QK_DIM
N_DIFF_HEADS = DIFF_WIDTH // DIFF_V_DIM
DIFF_QK_WIDTH = N_DIFF_HEADS * 2 * DIFF_QK_DIM
IN_WIDTH = 3 * DIL_WIDTH + 2 * DIFF_QK_WIDTH + DIFF_WIDTH
ROPE_THETA = 500000.0
ROPE_FRACTION = 4
PEER_HEADS = 8
PEER_N_KEYS = 128
PEER_N_EXPERTS = PEER_N_KEYS * PEER_N_KEYS
PEER_KEY_DIM = 256
PEER_TOPK = 16
LN_EPS = 1e-5
NEG_BIG = -1e30
DEEPNORM_ALPHA = (2.0 * DEPTH) ** 0.25

LANES = 128
BF16_ROWS = 16
VMEM_LIMIT = 56 * 1024 * 1024

COL_QA = 0
COL_KA = DIL_WIDTH // LANES
COL_VA = 2 * DIL_WIDTH // LANES
COL_QD = 3 * DIL_WIDTH // LANES
COL_KD = COL_QD + DIFF_QK_WIDTH // LANES
COL_VD = COL_KD + DIFF_QK_WIDTH // LANES


def _cparams(*sem):
    return pltpu.CompilerParams(dimension_semantics=sem, vmem_limit_bytes=VMEM_LIMIT)


def _layer_norm(x, g, b):
    mu = jnp.mean(x, -1, keepdims=True)
    xc = x - mu
    var = jnp.mean(xc * xc, -1, keepdims=True)
    return xc * lax.rsqrt(var + LN_EPS) * g + b


def _head_rms(t, g):
    return t * lax.rsqrt(jnp.mean(t * t, -1, keepdims=True) + LN_EPS) * g


def _dot_nt(a, b):
    return lax.dot_general(a, b, (((1,), (1,)), ((), ())), preferred_element_type=F32)


INPROJ_TM = 1024
INPROJ_TN = 512
ROPE_DIL_SHIFT = HEAD_DIM // ROPE_FRACTION // 2
ROPE_DIFF_SHIFT = DIFF_QK_DIM // ROPE_FRACTION // 2


def _rope_tables(seq, period, half):
    rot = 2 * half
    inv_freq = 1.0 / (ROPE_THETA ** (jnp.arange(0, rot, 2, dtype=F32) / rot))
    ang = jnp.arange(seq, dtype=F32)[:, None] * inv_freq[None, :]
    cos, sin = jnp.cos(ang), jnp.sin(ang)
    pad = period - rot
    c = jnp.concatenate([cos, cos, jnp.ones((seq, pad), F32)], axis=1)
    a = jnp.concatenate([-sin, jnp.zeros((seq, half + pad), F32)], axis=1)
    b = jnp.concatenate([jnp.zeros((seq, half), F32), sin, jnp.zeros((seq, pad), F32)], axis=1)
    reps = LANES // period
    return tuple(jnp.tile(t, (1, reps)) for t in (c, a, b))


ROPE_KIND_DIL, ROPE_KIND_DIFF, ROPE_KIND_NONE = 0, 1, 2


def _inproj_kernel(x_ref, g_ref, b_ref, w_ref, c_ref, a_ref, bt_ref, h_ref, o_ref, xs_ref, raw_ref,
                   *, n_col_blocks, kind_of_block):
    s = pl.program_id(0)
    last_tile = pl.num_programs(0) - 2
    j = jnp.minimum(s, last_tile) % n_col_blocks
    j_prev = jnp.maximum(s - 1, 0) % n_col_blocks

    @pl.when(s == 0)
    def _():
        raw_ref[...] = jnp.zeros_like(raw_ref)

    @pl.when(jnp.logical_and(j == 0, s <= last_tile))
    def _():
        h = _layer_norm(x_ref[...], g_ref[...], b_ref[...])
        h_ref[...] = h
        xs_ref[...] = h.astype(BF16)

    shift = jnp.where(kind_of_block(j_prev) == ROPE_KIND_DIFF, ROPE_DIFF_SHIFT, ROPE_DIL_SHIFT)
    c, a, b = c_ref[...], a_ref[...], bt_ref[...]
    for gidx in range(raw_ref.shape[1] // LANES):
        t = raw_ref[:, gidx * LANES:(gidx + 1) * LANES]
        r = t * c + pltpu.roll(t, LANES - shift, 1) * a + pltpu.roll(t, shift, 1) * b
        o_ref[:, gidx * LANES:(gidx + 1) * LANES] = r.astype(o_ref.dtype)

    raw_ref[...] = jnp.dot(xs_ref[...], w_ref[...], preferred_element_type=F32)


def _inproj(x2, g, b, w16, seq):
    t = x2.shape[0]
    tm, tn = INPROJ_TM, INPROJ_TN
    assert t % tm == 0 and seq % tm == 0 and IN_WIDTH % tn == 0
    assert (2 * DIL_WIDTH) % tn == 0 and (3 * DIL_WIDTH) % tn == 0 and DIFF_QK_WIDTH * 2 % tn == 0
    sblocks = seq // tm
    n_col = IN_WIDTH // tn
    n_tiles = (t // tm) * n_col
    n_dil, diff_lo = 2 * DIL_WIDTH // tn, 3 * DIL_WIDTH // tn
    diff_hi = (3 * DIL_WIDTH + 2 * DIFF_QK_WIDTH) // tn

    def kind_of_block(j):
        return jnp.where(j < n_dil, ROPE_KIND_DIL,
                         jnp.where(jnp.logical_and(j >= diff_lo, j < diff_hi), ROPE_KIND_DIFF, ROPE_KIND_NONE))

    def tile(s):
        s = jnp.minimum(s, n_tiles - 1)
        return s // n_col, s % n_col

    def prev_tile(s):
        return tile(jnp.maximum(s - 1, 0))

    ident = (jnp.ones((seq, LANES), F32), jnp.zeros((seq, LANES), F32), jnp.zeros((seq, LANES), F32))
    tabs = [jnp.stack(per_kind) for per_kind in zip(_rope_tables(seq, HEAD_DIM, ROPE_DIL_SHIFT),
                                                     _rope_tables(seq, DIFF_QK_DIM, ROPE_DIFF_SHIFT), ident)]
    tab_spec = pl.BlockSpec(
        (None, tm, LANES), lambda s: (kind_of_block(prev_tile(s)[1]), prev_tile(s)[0] % sblocks, 0))
    return pl.pallas_call(
        functools.partial(_inproj_kernel, n_col_blocks=n_col, kind_of_block=kind_of_block),
        grid=(n_tiles + 1,),
        in_specs=[
            pl.BlockSpec((tm, D_MODEL), lambda s: (tile(s)[0], 0)),
            pl.BlockSpec((1, D_MODEL), lambda s: (0, 0)),
            pl.BlockSpec((1, D_MODEL), lambda s: (0, 0)),
            pl.BlockSpec((D_MODEL, tn), lambda s: (0, tile(s)[1])),
            tab_spec, tab_spec, tab_spec,
        ],
        out_specs=[
            pl.BlockSpec((tm, D_MODEL), lambda s: (tile(s)[0], 0)),
            pl.BlockSpec((tm, tn), lambda s: prev_tile(s)),
        ],
        out_shape=[
            jax.ShapeDtypeStruct((t, D_MODEL), F32),
            jax.ShapeDtypeStruct((t, IN_WIDTH), BF16),
        ],
        scratch_shapes=[pltpu.VMEM((tm, D_MODEL), BF16), pltpu.VMEM((tm, tn), F32)],
        compiler_params=_cparams("arbitrary"),
        name="inproj",
    )(x2, g, b, w16, *tabs)


DIL_BQ = 128
DIL_WIN = 256
DIL_GROUP = 8


def _dil_plan(seq, window, d):
    length = seq // d
    half = window // (2 * d)
    bq = min(DIL_BQ, length)
    win = min(DIL_WIN, length)
    nblk = length // bq
    blocks = min(nblk, DIL_GROUP)
    residues = max(1, min(d, DIL_GROUP // nblk))
    assert seq % d == 0 and length % bq == 0 and nblk % blocks == 0 and d % residues == 0
    assert win >= min(length, bq + 2 * half) and half % BF16_ROWS == 0
    return length, half, bq, win, nblk, blocks, residues


def _dil_attn_kernel(q_ref, k_ref, v_ref, g_ref, o_ref,
                     q32, k32, v32, qd, kd, vd, vaug, acc, m_s, l_s, *, seq):
    scale = HEAD_DIM ** -0.5
    q32[...] = q_ref[...].astype(F32)
    k32[...] = k_ref[...].astype(F32)
    v32[...] = v_ref[...].astype(F32)
    vaug[:, :LANES] = v_ref[...]
    vaug[:, LANES:] = jnp.ones((seq, LANES), BF16)
    vd[:, LANES:] = jnp.ones((vd.shape[0], LANES), BF16)

    def run_config(first, window, d):
        length, half, bq, win, nblk, blocks, residues = _dil_plan(seq, window, d)

        def group(rg, bg):
            if d == 1:
                qs, ks, vs = q_ref, k_ref, vaug
            else:
                qs, ks, vs = qd, kd, vd
            items = []
            for j in range(residues):
                r = rg * residues + j
                for u in range(blocks):
                    q0 = pl.multiple_of((bg * blocks + u) * bq, bq)
                    k0 = pl.multiple_of(jnp.clip(q0 - half, 0, length - win), BF16_ROWS)
                    rows = pl.ds(q0, bq) if d == 1 else pl.ds(r + d * q0, bq, stride=d)
                    items.append((j * length, q0, k0, rows))
            scores = []
            for base, q0, k0, _ in items:
                s = _dot_nt(qs[pl.ds(base + q0, bq), :], ks[pl.ds(base + k0, win), :]) * scale
                qpos = q0 + lax.broadcasted_iota(jnp.int32, (bq, win), 0)
                kpos = k0 + lax.broadcasted_iota(jnp.int32, (bq, win), 1)
                scores.append(jnp.where(jnp.abs(qpos - kpos) <= half, s, NEG_BIG))
            m_new = [jnp.broadcast_to(jnp.max(s, -1, keepdims=True), (bq, LANES)) for s in scores]
            if not first:
                m_old = [m_s[rows, :] for _, _, _, rows in items]
                m_new = [jnp.maximum(a, b) for a, b in zip(m_old, m_new)]
            def widen(m):
                return jnp.tile(m, (1, win // LANES)) if win % LANES == 0 else m[:, :1]
            probs = [jnp.exp(s - widen(m)).astype(BF16) for s, m in zip(scores, m_new)]
            pv = [jnp.dot(p, vs[pl.ds(base + k0, win), :], preferred_element_type=F32)
                  for p, (base, _, k0, _) in zip(probs, items)]
            for idx, (_, _, _, rows) in enumerate(items):
                a_new, l_new = pv[idx][:, :LANES], pv[idx][:, LANES:]
                if not first:
                    alpha = jnp.exp(m_old[idx] - m_new[idx])
                    a_new = alpha * acc[rows, :] + a_new
                    l_new = alpha * l_s[rows, :] + l_new
                acc[rows, :] = a_new
                m_s[rows, :] = m_new[idx]
                l_s[rows, :] = l_new

        def residue_group(rg, carry):
            if d > 1:
                for j in range(residues):
                    sub = pl.ds(rg * residues + j, length, stride=d)
                    dst = pl.ds(j * length, length)
                    qd[dst, :] = q32[sub, :].astype(BF16)
                    kd[dst, :] = k32[sub, :].astype(BF16)
                    vd[dst, :LANES] = v32[sub, :].astype(BF16)
            if nblk == blocks:
                group(rg, 0)
            else:
                lax.fori_loop(0, nblk // blocks, lambda bg, c: (group(rg, bg), c)[1], 0)
            return carry

        if d == residues:
            residue_group(0, 0)
        else:
            lax.fori_loop(0, d // residues, residue_group, 0)

    for ci, (window, d) in enumerate(DIL_CONFIGS):
        run_config(ci == 0, window, d)

    o_ref[...] = _head_rms(acc[...] / l_s[...], g_ref[...]).astype(o_ref.dtype)


def _dil_attn(proj3, g):
    bsz, seq, _ = proj3.shape
    assert DIL_CONFIGS[0][1] == 1
    plans = [_dil_plan(seq, window, d) for window, d in DIL_CONFIGS if d > 1]
    sub_rows = max(length * residues for length, _, _, _, _, _, residues in plans)

    def spec(col0):
        return pl.BlockSpec((None, seq, LANES), lambda b, h: (b, 0, col0 + h))

    return pl.pallas_call(
        functools.partial(_dil_attn_kernel, seq=seq),
        grid=(bsz, N_DIL_HEADS),
        in_specs=[spec(COL_QA), spec(COL_KA), spec(COL_VA), pl.BlockSpec((1, LANES), lambda b, h: (0, 0))],
        out_specs=pl.BlockSpec((None, seq, LANES), lambda b, h: (b, 0, h)),
        out_shape=jax.ShapeDtypeStruct((bsz, seq, DIL_WIDTH), BF16),
        scratch_shapes=[pltpu.VMEM((seq, LANES), F32)] * 3
        + [pltpu.VMEM((sub_rows, LANES), BF16)] * 2
        + [pltpu.VMEM((sub_rows, 2 * LANES), BF16), pltpu.VMEM((seq, 2 * LANES), BF16)]
        + [pltpu.VMEM((seq, LANES), F32)] * 3,
        compiler_params=_cparams("parallel", "parallel"),
        name="dil_attn",
    )(proj3, proj3, proj3, g)


DIFF_TQ = 256
DIFF_KEY_CHUNK = 1024


def _diff_attn_kernel(q_ref, k_ref, v_ref, lq1, lk1, lq2, lk2, g_ref, o_ref, vaug, s_scr, mx_scr, *, lam_init):
    seq = k_ref.shape[0]
    kc = min(DIFF_KEY_CHUNK, seq)
    assert seq % kc == 0
    chunks = [pl.ds(c * kc, kc) for c in range(seq // kc)]

    @pl.when(pl.program_id(2) == 0)
    def _():
        vaug[:, :LANES] = v_ref[...]
        vaug[:, LANES:] = jnp.ones((seq, LANES), BF16)
        s_scr[...] = jnp.zeros_like(s_scr)
        mx_scr[...] = jnp.zeros_like(mx_scr)

    scale = DIFF_QK_DIM ** -0.5
    assert 2.0 ** round(math.log2(scale)) == scale
    q = q_ref[...] * scale
    lo = lax.broadcasted_iota(jnp.int32, q.shape, 1) < DIFF_QK_DIM
    zero = jnp.zeros_like(q)
    q_maps = (jnp.where(lo, q, zero), jnp.where(lo, zero, q))

    outs = []
    for mp in range(2):
        m_prev = mx_scr[mp][:, :1]
        pv, m_next = None, None
        for c in chunks:
            p = jnp.exp(s_scr[mp, :, c] - m_prev).astype(BF16)
            part = jnp.dot(p, vaug[c, :], preferred_element_type=F32)
            pv = part if pv is None else pv + part
            s_new = _dot_nt(q_maps[mp], k_ref[c, :])
            s_scr[mp, :, c] = s_new
            m_chunk = jnp.max(s_new, -1, keepdims=True)
            m_next = m_chunk if m_next is None else jnp.maximum(m_next, m_chunk)
        mx_scr[mp] = jnp.broadcast_to(m_next, mx_scr.shape[1:])
        outs.append(pv[:, :LANES] / pv[:, LANES:])
    o0, o1 = outs
    lam = (jnp.exp(jnp.sum(lq1[...] * lk1[...], keepdims=True))
           - jnp.exp(jnp.sum(lq2[...] * lk2[...], keepdims=True)) + lam_init)
    a = o0 - lam * o1
    o_ref[...] = (_head_rms(a, g_ref[...]) * (1.0 - lam_init)).astype(o_ref.dtype)


def _diff_attn(proj3, lq1, lk1, lq2, lk2, g, lam_init):
    bsz, seq, _ = proj3.shape
    tq = DIFF_TQ
    assert seq % tq == 0
    n_tiles = seq // tq
    vec = pl.BlockSpec((1, DIFF_QK_DIM), lambda b, h, i: (0, 0))
    return pl.pallas_call(
        functools.partial(_diff_attn_kernel, lam_init=lam_init),
        grid=(bsz, N_DIFF_HEADS, n_tiles + 1),
        in_specs=[
            pl.BlockSpec((None, tq, LANES), lambda b, h, i: (b, jnp.minimum(i, n_tiles - 1), COL_QD + h)),
            pl.BlockSpec((None, seq, LANES), lambda b, h, i: (b, 0, COL_KD + h)),
            pl.BlockSpec((None, seq, LANES), lambda b, h, i: (b, 0, COL_VD + h)),
            vec, vec, vec, vec,
            pl.BlockSpec((1, DIFF_V_DIM), lambda b, h, i: (0, 0)),
        ],
        out_specs=pl.BlockSpec((None, tq, LANES), lambda b, h, i: (b, jnp.maximum(i - 1, 0), h)),
        out_shape=jax.ShapeDtypeStruct((bsz, seq, DIFF_WIDTH), BF16),
        scratch_shapes=[pltpu.VMEM((seq, 2 * LANES), BF16), pltpu.VMEM((2, tq, seq), F32),
                        pltpu.VMEM((2, tq, LANES), F32)],
        compiler_params=_cparams("parallel", "parallel", "arbitrary"),
        name="diff_attn",
    )(proj3, proj3, proj3, lq1, lk1, lq2, lk2, g)


OUTPROJ_TM = 256


def _outproj_kernel(ma_ref, md_ref, wa_ref, wd_ref, h_ref, g_ref, b_ref, o32_ref, o16_ref, o16t_ref):
    mix = jnp.dot(ma_ref[...], wa_ref[...], preferred_element_type=F32)
    mix = mix + jnp.dot(md_ref[...], wd_ref[...], preferred_element_type=F32)
    h = _layer_norm(DEEPNORM_ALPHA * h_ref[...] + mix, g_ref[...], b_ref[...])
    o32_ref[...] = h
    o16_ref[...] = h.astype(BF16)
    o16t_ref[...] = h.T.astype(BF16)


def _outproj(mix_a, mix_d, w_a, w_d, h0, g, b):
    t = h0.shape[0]
    tm = OUTPROJ_TM
    assert t % tm == 0
    row = pl.BlockSpec((1, D_MODEL), lambda i: (0, 0))
    return pl.pallas_call(
        _outproj_kernel,
        grid=(t // tm,),
        in_specs=[
            pl.BlockSpec((tm, DIL_WIDTH), lambda i: (i, 0)),
            pl.BlockSpec((tm, DIFF_WIDTH), lambda i: (i, 0)),
            pl.BlockSpec((DIL_WIDTH, D_MODEL), lambda i: (0, 0)),
            pl.BlockSpec((DIFF_WIDTH, D_MODEL), lambda i: (0, 0)),
            pl.BlockSpec((tm, D_MODEL), lambda i: (i, 0)),
            row, row,
        ],
        out_specs=[pl.BlockSpec((tm, D_MODEL), lambda i: (i, 0))] * 2
        + [pl.BlockSpec((D_MODEL, tm), lambda i: (0, i))],
        out_shape=[jax.ShapeDtypeStruct((t, D_MODEL), F32), jax.ShapeDtypeStruct((t, D_MODEL), BF16),
                   jax.ShapeDtypeStruct((D_MODEL, t), BF16)],
        compiler_params=_cparams("parallel"),
        name="outproj",
    )(mix_a, mix_d, w_a, w_d, h0, g, b)


SCORE_TM = 512
N_KEYSETS = 2 * PEER_HEADS
HALF_KEY_DIM = PEER_KEY_DIM // 2


def _peer_score_kernel(h_ref, wq_ref, keys_ref, s_ref):
    q = jnp.dot(h_ref[...], wq_ref[...], preferred_element_type=F32).astype(BF16)
    for hc in range(N_KEYSETS):
        qs = q[:, hc * HALF_KEY_DIM:(hc + 1) * HALF_KEY_DIM]
        s_ref[hc] = _dot_nt(keys_ref[hc], qs)


def _peer_scores(h16, wq16, keys16):
    t = h16.shape[0]
    tm = SCORE_TM
    assert t % tm == 0 and HALF_KEY_DIM == LANES
    return pl.pallas_call(
        _peer_score_kernel,
        grid=(t // tm,),
        in_specs=[
            pl.BlockSpec((tm, D_MODEL), lambda i: (i, 0)),
            pl.BlockSpec((D_MODEL, PEER_HEADS * PEER_KEY_DIM), lambda i: (0, 0)),
            pl.BlockSpec((N_KEYSETS, PEER_N_KEYS, HALF_KEY_DIM), lambda i: (0, 0, 0)),
        ],
        out_specs=pl.BlockSpec((N_KEYSETS, PEER_N_KEYS, tm), lambda i: (0, 0, i)),
        out_shape=jax.ShapeDtypeStruct((N_KEYSETS, PEER_N_KEYS, t), F32),
        compiler_params=_cparams("parallel"),
        name="peer_scores",
    )(h16, wq16, keys16)


ROUTE_TT = 512


SUBLANES = 8


def _sort_network(n):
    def merge(lo, hi, r):
        step = r * 2
        if step < hi - lo:
            yield from merge(lo, hi, step)
            yield from merge(lo + r, hi, step)
            yield from ((i, i + r) for i in range(lo + r, hi - r, step))
        else:
            yield (lo, lo + r)

    def sort(lo, hi):
        if hi - lo >= 1:
            mid = lo + (hi - lo) // 2
            yield from sort(lo, mid)
            yield from sort(mid + 1, hi)
            yield from merge(lo, hi, 1)

    return list(sort(0, n - 1))


def _bitonic_merge_network(n):
    pairs, j = [], n // 2
    while j >= 1:
        pairs += [(i, i ^ j) for i in range(n) if i ^ j > i]
        j //= 2
    return pairs


def _compare_exchange_desc(xs, pairs):
    xs = list(xs)
    for lo, hi in pairs:
        xs[lo], xs[hi] = jnp.maximum(xs[lo], xs[hi]), jnp.minimum(xs[lo], xs[hi])
    return xs


def _top_sorted(tiles):
    k = len(tiles)
    top = _compare_exchange_desc(tiles, _sort_network(k))
    shift = 1
    while shift < SUBLANES:
        other = [pltpu.roll(x, shift, 0) for x in top]
        top = [jnp.maximum(top[i], other[k - 1 - i]) for i in range(k)]
        top = _compare_exchange_desc(top, _bitonic_merge_network(k))
        shift *= 2
    return top


def _count_greater(x, top):
    assert len(top) == 16
    b3 = top[7] > x
    b2 = jnp.where(b3, top[11], top[3]) > x
    b1 = jnp.where(b3, jnp.where(b2, top[13], top[9]), jnp.where(b2, top[5], top[1])) > x
    q = [jnp.where(b1, top[4 * i + 2], top[4 * i]) for i in range(4)]
    b0 = jnp.where(b3, jnp.where(b2, q[3], q[2]), jnp.where(b2, q[1], q[0])) > x
    count = (jnp.where(b3, 8.0, 0.0) + jnp.where(b2, 4.0, 0.0)) + (jnp.where(b1, 2.0, 0.0) + jnp.where(b0, 1.0, 0.0))
    return jnp.where(top[15] > x, 16.0, count)


def _top_ranks(v, k):
    n = v.shape[0]
    rows = lax.broadcasted_iota(jnp.int32, v.shape, 0)
    rank = jnp.full(v.shape, k, jnp.int32)
    work = v
    vals = []
    for r in range(k):
        m = jnp.max(work, axis=0, keepdims=True)
        first = jnp.min(jnp.where(work == m, rows, n), axis=0, keepdims=True)
        sel = rows == first
        rank = jnp.where(sel, r, rank)
        work = jnp.where(sel, -jnp.inf, work)
        vals.append(m)
    return jnp.concatenate(vals, axis=0), rank


def _staircase(sv0, sv1, k):
    arow = lax.broadcasted_iota(jnp.int32, sv0.shape, 0)
    count = jnp.zeros(sv0.shape, jnp.int32)
    front = sv0 + sv1[0:1]
    best = front[0:1]
    z = jnp.zeros_like(best)
    for _ in range(k):
        m = jnp.max(front, axis=0, keepdims=True)
        first = jnp.min(jnp.where(front == m, arow, k), axis=0, keepdims=True)
        win = arow == first
        z = z + jnp.exp(m - best)
        count = count + win.astype(jnp.int32)
        nw = jnp.sum(jnp.where(win, count, 0), axis=0, keepdims=True)
        nxt = jnp.sum(jnp.where(arow == nw, sv1, 0.0), axis=0, keepdims=True)
        nxt = jnp.where(nw < k, nxt, -jnp.inf)
        front = jnp.where(win, sv0 + nxt, front)
    return count.astype(F32), z


def _peer_route_kernel(s_ref, n_ref, e0_ref, r1_ref, e1_ref, sv0_s, rank0_s, sv1_s, rank1_s, cnt_s, zinv_s):
    k = PEER_TOPK
    n_keys, tt = s_ref.shape[1], s_ref.shape[2]
    n_tiles = n_keys // SUBLANES
    assert n_tiles == k
    sides = ((sv0_s, rank0_s), (sv1_s, rank1_s))

    def lane_col(c):
        return pl.ds(pl.multiple_of(c * LANES, LANES), LANES)

    def tiles_of(side, col):
        return [s_ref[side, g * SUBLANES:(g + 1) * SUBLANES, col] for g in range(n_tiles)]

    def sort_col(c, tied):
        col = lane_col(c)
        for side, (sv_s, _) in enumerate(sides):
            tiles = tiles_of(side, col)
            top = _top_sorted(tiles)
            for r in range(k):
                sv_s[r:r + 1, col] = top[r][0:1]
            equal_pair = functools.reduce(jnp.logical_or, [top[r] == top[r + 1] for r in range(k - 1)])
            at_least = sum(jnp.where(x >= top[k - 1], 1.0, 0.0) for x in tiles)
            bad = jnp.where(equal_pair, 1.0, 0.0) + jnp.abs(jnp.sum(at_least, axis=0, keepdims=True) - k)
            tied = jnp.maximum(tied, jnp.max(bad))
            if side == 1:
                for g, x in enumerate(tiles):
                    r1_ref[g * SUBLANES:(g + 1) * SUBLANES, col] = _count_greater(x, top).astype(r1_ref.dtype)
        return tied

    tied = lax.fori_loop(0, tt // LANES, sort_col, jnp.zeros((), F32)) > 0

    @pl.when(tied)
    def _():
        for side, (sv_s, rank_s) in enumerate(sides):
            sv, rank = _top_ranks(s_ref[side], k)
            sv_s[...] = sv
            rank_s[...] = rank
        r1_ref[...] = rank1_s[...].astype(F32).astype(r1_ref.dtype)

    count, z = _staircase(sv0_s[...], sv1_s[...], k)
    cnt_s[...] = count
    zinv_s[...] = 1.0 / z

    def emit_col(c, carry):
        col = lane_col(c)
        s0 = s_ref[0, :, col]
        e0_ref[:, col] = jnp.exp(s0 - sv0_s[0:1, col])
        e1_ref[:, col] = (jnp.exp(s_ref[1, :, col] - sv1_s[0:1, col]) * zinv_s[:, col]).astype(e1_ref.dtype)
        n_of_row = jnp.zeros(s0.shape, F32)
        for a in range(k):
            n_of_row = jnp.where(s0 == sv0_s[a:a + 1, col], cnt_s[a:a + 1, col], n_of_row)
        n_ref[:, col] = n_of_row
        return carry

    lax.fori_loop(0, tt // LANES, emit_col, 0)

    @pl.when(tied)
    def _():
        rank0 = rank0_s[...]
        cnt = cnt_s[...]
        n_of_row = jnp.zeros(rank0.shape, F32)
        for a in range(k):
            n_of_row = jnp.where(rank0 == a, cnt[a:a + 1], n_of_row)
        n_ref[...] = n_of_row


def _peer_route(scores):
    t = scores.shape[-1]
    tt = ROUTE_TT
    assert t % tt == 0 and tt % LANES == 0
    out_spec = pl.BlockSpec((None, PEER_N_KEYS, tt), lambda h, i: (h, 0, i))
    shape = (PEER_HEADS, PEER_N_KEYS, t)
    return pl.pallas_call(
        _peer_route_kernel,
        grid=(PEER_HEADS, t // tt),
        in_specs=[pl.BlockSpec((2, PEER_N_KEYS, tt), lambda h, i: (h, 0, i))],
        out_specs=[out_spec] * 4,
        out_shape=[jax.ShapeDtypeStruct(shape, F32), jax.ShapeDtypeStruct(shape, F32),
                   jax.ShapeDtypeStruct(shape, BF16), jax.ShapeDtypeStruct(shape, BF16)],
        scratch_shapes=[pltpu.VMEM((PEER_TOPK, tt), F32), pltpu.VMEM((PEER_N_KEYS, tt), jnp.int32),
                        pltpu.VMEM((PEER_TOPK, tt), F32), pltpu.VMEM((PEER_N_KEYS, tt), jnp.int32),
                        pltpu.VMEM((PEER_TOPK, tt), F32), pltpu.VMEM((1, tt), F32)],
        compiler_params=_cparams("parallel", "parallel"),
        name="peer_route",
    )(scores)


EXPERT_TM = 512
EXPERT_TE = 512
EXPERT_DRAIN_STEPS = 2
INV_SQRT2 = 1.0 / math.sqrt(2.0)


def _expert_work(s, lag, n_work):
    return jnp.clip(s - lag, 0, n_work - 1)


def _peer_expert_kernel(xt_ref, u_ref, vt_ref, n_ref, e0_ref, r1_ref, e1_ref, h_ref, g_ref, b_ref,
                        o_ref, acc_ref, pre_ref, hid_ref, *, n_blocks, n_work):
    s = pl.program_id(0)
    te = u_ref.shape[0]
    tm = xt_ref.shape[1]
    rows_per_block = te // PEER_N_KEYS
    zero = jnp.zeros((), BF16)
    block_down = _expert_work(s, 2, n_work) % n_blocks
    block_gate = _expert_work(s, 1, n_work) % n_blocks

    @pl.when(s == 0)
    def _():
        pre_ref[...] = jnp.zeros_like(pre_ref)
        hid_ref[...] = jnp.zeros_like(hid_ref)

    @pl.when(block_down == 0)
    def _():
        acc_ref[...] = jnp.zeros_like(acc_ref)

    acc_ref[...] += jnp.dot(vt_ref[...], hid_ref[...], preferred_element_type=F32)

    a = pre_ref[...]
    act = (0.5 * a * (1.0 + lax.erf(a * INV_SQRT2))).astype(BF16)
    first_row = block_gate * rows_per_block
    gates = []
    for ib in range(rows_per_block):
        i = first_row + ib
        w = jnp.zeros((PEER_N_KEYS, tm), BF16)
        for hd in range(PEER_HEADS):
            n_b = jnp.broadcast_to(n_ref[hd, pl.ds(i, 1), :], (PEER_N_KEYS, tm)).astype(BF16)
            e0_b = jnp.broadcast_to(e0_ref[hd, pl.ds(i, 1), :], (PEER_N_KEYS, tm)).astype(BF16)
            w = w + jnp.where(r1_ref[hd] < n_b, e1_ref[hd], zero) * e0_b
        gates.append(w)
    hid_ref[...] = jnp.concatenate(gates, axis=0) * act

    pre_ref[...] = jnp.dot(u_ref[...], xt_ref[...], preferred_element_type=F32)

    @pl.when(jnp.logical_and(block_down == n_blocks - 1, s >= EXPERT_DRAIN_STEPS))
    def _():
        y = DEEPNORM_ALPHA * h_ref[...] + acc_ref[...].T
        o_ref[...] = _layer_norm(y, g_ref[...], b_ref[...])


def _peer_experts(h16t, u16, v16t, n_t, e0_t, r1_t, e1_t, h32, g, b):
    t = h16t.shape[1]
    tm, te = EXPERT_TM, EXPERT_TE
    assert t % tm == 0 and PEER_N_EXPERTS % te == 0 and te % PEER_N_KEYS == 0
    n_blocks = PEER_N_EXPERTS // te
    n_work = (t // tm) * n_blocks

    def item(lag):
        return lambda s: divmod(_expert_work(s, lag, n_work), n_blocks)

    up, gate, down = item(0), item(1), item(2)
    route = pl.BlockSpec((PEER_HEADS, PEER_N_KEYS, tm), lambda s: (0, 0, gate(s)[0]))
    row = pl.BlockSpec((1, D_MODEL), lambda s: (0, 0))
    return pl.pallas_call(
        functools.partial(_peer_expert_kernel, n_blocks=n_blocks, n_work=n_work),
        grid=(n_work + EXPERT_DRAIN_STEPS,),
        in_specs=[
            pl.BlockSpec((D_MODEL, tm), lambda s: (0, up(s)[0])),
            pl.BlockSpec((te, D_MODEL), lambda s: (up(s)[1], 0)),
            pl.BlockSpec((None, D_MODEL, te), lambda s: (down(s)[1], 0, 0)),
            route, route, route, route,
            pl.BlockSpec((tm, D_MODEL), lambda s: (down(s)[0], 0)),
            row, row,
        ],
        out_specs=pl.BlockSpec((tm, D_MODEL), lambda s: (down(s)[0], 0)),
        out_shape=jax.ShapeDtypeStruct((t, D_MODEL), F32),
        scratch_shapes=[pltpu.VMEM((D_MODEL, tm), F32), pltpu.VMEM((te, tm), F32), pltpu.VMEM((te, tm), BF16)],
        compiler_params=_cparams("arbitrary"),
        name="peer_experts",
    )(h16t, u16, v16t, n_t, e0_t, r1_t, e1_t, h32, g, b)


def kernel(x, ln_emb_g, ln_emb_b, w_in, dil_norm_g, lambda_q1, lambda_k1, lambda_q2, lambda_k2, subln_g,
           w_out, ln1_g, ln1_b, peer_w_query, peer_sub_keys, peer_u, peer_v, ln2_g, ln2_b):
    bsz, seq, d_model = x.shape
    assert d_model == D_MODEL and w_in.shape[0] == DEPTH == 1
    t = bsz * seq
    row = lambda p: p.reshape(1, -1).astype(F32)
    lam_init = 0.8 - 0.6 * math.exp(-0.3 * 0)

    h0, proj = _inproj(x.reshape(t, D_MODEL), row(ln_emb_g), row(ln_emb_b), w_in[0].astype(BF16), seq)
    proj3 = proj.reshape(bsz, seq, IN_WIDTH)

    mix_a = _dil_attn(proj3, row(dil_norm_g[0])).reshape(t, DIL_WIDTH)
    mix_d = _diff_attn(proj3, row(lambda_q1[0]), row(lambda_k1[0]), row(lambda_q2[0]), row(lambda_k2[0]),
                       row(subln_g[0]), lam_init).reshape(t, DIFF_WIDTH)

    w_out16 = w_out[0].astype(BF16)
    h1, h1_16, h1_16t = _outproj(mix_a, mix_d, w_out16[:DIL_WIDTH], w_out16[DIL_WIDTH:], h0,
                                 row(ln1_g[0]), row(ln1_b[0]))

    keys16 = peer_sub_keys[0].reshape(N_KEYSETS, PEER_N_KEYS, HALF_KEY_DIM).astype(BF16)
    scores = _peer_scores(h1_16, peer_w_query[0].astype(BF16), keys16)
    n_t, e0_t, r1_t, e1_t = _peer_route(scores)
    v16t = peer_v[0].astype(BF16).reshape(PEER_N_EXPERTS // EXPERT_TE, EXPERT_TE, D_MODEL).transpose(0, 2, 1)
    out = _peer_experts(h1_16t, peer_u[0].astype(BF16), v16t, n_t, e0_t, r1_t, e1_t,
                        h1, row(ln2_g[0]), row(ln2_b[0]))
    return out.reshape(bsz, seq, D_MODEL)
```

```python
import functools
import math

import jax
import jax.numpy as jnp
from jax import lax
from jax.experimental import pallas as pl
from jax.experimental.pallas import tpu as pltpu

F32 = jnp.float32
BF16 = jnp.bfloat16

D_MODEL = 2048
DEPTH = 1
HEAD_DIM = 128
DIFF_WIDTH = D_MODEL // 4
DIL_WIDTH = D_MODEL - DIFF_WIDTH
N_DIL_HEADS = DIL_WIDTH // HEAD_DIM
DIL_CONFIGS = ((128, 1), (512, 4), (2048, 16))
DIFF_QK_DIM = 64
DIFF_V_DIM = 2 * DIFF_QK_DIM
N_DIFF_HEADS = DIFF_WIDTH // DIFF_V_DIM
DIFF_QK_WIDTH = N_DIFF_HEADS * 2 * DIFF_QK_DIM
IN_WIDTH = 3 * DIL_WIDTH + 2 * DIFF_QK_WIDTH + DIFF_WIDTH
ROPE_THETA = 500000.0
ROPE_FRACTION = 4
PEER_HEADS = 8
PEER_N_KEYS = 128
PEER_N_EXPERTS = PEER_N_KEYS * PEER_N_KEYS
PEER_KEY_DIM = 256
PEER_TOPK = 16
LN_EPS = 1e-5
NEG_BIG = -1e30
DEEPNORM_ALPHA = (2.0 * DEPTH) ** 0.25

LANES = 128
BF16_ROWS = 16
VMEM_LIMIT = 56 * 1024 * 1024

COL_QA = 0
COL_KA = DIL_WIDTH // LANES
COL_VA = 2 * DIL_WIDTH // LANES
COL_QD = 3 * DIL_WIDTH // LANES
COL_KD = COL_QD + DIFF_QK_WIDTH // LANES
COL_VD = COL_KD + DIFF_QK_WIDTH // LANES


def _cparams(*sem):
    return pltpu.CompilerParams(dimension_semantics=sem, vmem_limit_bytes=VMEM_LIMIT)


def _layer_norm(x, g, b):
    mu = jnp.mean(x, -1, keepdims=True)
    xc = x - mu
    var = jnp.mean(xc * xc, -1, keepdims=True)
    return xc * lax.rsqrt(var + LN_EPS) * g + b


def _head_rms(t, g):
    return t * lax.rsqrt(jnp.mean(t * t, -1, keepdims=True) + LN_EPS) * g


def _dot_nt(a, b):
    return lax.dot_general(a, b, (((1,), (1,)), ((), ())), preferred_element_type=F32)


INPROJ_TM = 1024
INPROJ_TN = 512
ROPE_DIL_SHIFT = HEAD_DIM // ROPE_FRACTION // 2
ROPE_DIFF_SHIFT = DIFF_QK_DIM // ROPE_FRACTION // 2


def _rope_tables(seq, period, half):
    rot = 2 * half
    inv_freq = 1.0 / (ROPE_THETA ** (jnp.arange(0, rot, 2, dtype=F32) / rot))
    ang = jnp.arange(seq, dtype=F32)[:, None] * inv_freq[None, :]
    cos, sin = jnp.cos(ang), jnp.sin(ang)
    pad = period - rot
    c = jnp.concatenate([cos, cos, jnp.ones((seq, pad), F32)], axis=1)
    a = jnp.concatenate([-sin, jnp.zeros((seq, half + pad), F32)], axis=1)
    b = jnp.concatenate([jnp.zeros((seq, half), F32), sin, jnp.zeros((seq, pad), F32)], axis=1)
    reps = LANES // period
    return tuple(jnp.tile(t, (1, reps)) for t in (c, a, b))


ROPE_KIND_DIL, ROPE_KIND_DIFF, ROPE_KIND_NONE = 0, 1, 2


def _inproj_kernel(x_ref, g_ref, b_ref, w_ref, c_ref, a_ref, bt_ref, h_ref, o_ref, xs_ref, raw_ref,
                   *, n_col_blocks, kind_of_block):
    s = pl.program_id(0)
    last_tile = pl.num_programs(0) - 2
    j = jnp.minimum(s, last_tile) % n_col_blocks
    j_prev = jnp.maximum(s - 1, 0) % n_col_blocks

    @pl.when(s == 0)
    def _():
        raw_ref[...] = jnp.zeros_like(raw_ref)

    @pl.when(jnp.logical_and(j == 0, s <= last_tile))
    def _():
        h = _layer_norm(x_ref[...], g_ref[...], b_ref[...])
        h_ref[...] = h
        xs_ref[...] = h.astype(BF16)

    shift = jnp.where(kind_of_block(j_prev) == ROPE_KIND_DIFF, ROPE_DIFF_SHIFT, ROPE_DIL_SHIFT)
    c, a, b = c_ref[...], a_ref[...], bt_ref[...]
    for gidx in range(raw_ref.shape[1] // LANES):
        t = raw_ref[:, gidx * LANES:(gidx + 1) * LANES]
        r = t * c + pltpu.roll(t, LANES - shift, 1) * a + pltpu.roll(t, shift, 1) * b
        o_ref[:, gidx * LANES:(gidx + 1) * LANES] = r.astype(o_ref.dtype)

    raw_ref[...] = jnp.dot(xs_ref[...], w_ref[...], preferred_element_type=F32)


def _inproj(x2, g, b, w16, seq):
    t = x2.shape[0]
    tm, tn = INPROJ_TM, INPROJ_TN
    assert t % tm == 0 and seq % tm == 0 and IN_WIDTH % tn == 0
    assert (2 * DIL_WIDTH) % tn == 0 and (3 * DIL_WIDTH) % tn == 0 and DIFF_QK_WIDTH * 2 % tn == 0
    sblocks = seq // tm
    n_col = IN_WIDTH // tn
    n_tiles = (t // tm) * n_col
    n_dil, diff_lo = 2 * DIL_WIDTH // tn, 3 * DIL_WIDTH // tn
    diff_hi = (3 * DIL_WIDTH + 2 * DIFF_QK_WIDTH) // tn

    def kind_of_block(j):
        return jnp.where(j < n_dil, ROPE_KIND_DIL,
                         jnp.where(jnp.logical_and(j >= diff_lo, j < diff_hi), ROPE_KIND_DIFF, ROPE_KIND_NONE))

    def tile(s):
        s = jnp.minimum(s, n_tiles - 1)
        return s // n_col, s % n_col

    def prev_tile(s):
        return tile(jnp.maximum(s - 1, 0))

    ident = (jnp.ones((seq, LANES), F32), jnp.zeros((seq, LANES), F32), jnp.zeros((seq, LANES), F32))
    tabs = [jnp.stack(per_kind) for per_kind in zip(_rope_tables(seq, HEAD_DIM, ROPE_DIL_SHIFT),
                                                     _rope_tables(seq, DIFF_QK_DIM, ROPE_DIFF_SHIFT), ident)]
    tab_spec = pl.BlockSpec(
        (None, tm, LANES), lambda s: (kind_of_block(prev_tile(s)[1]), prev_tile(s)[0] % sblocks, 0))
    return pl.pallas_call(
        functools.partial(_inproj_kernel, n_col_blocks=n_col, kind_of_block=kind_of_block),
        grid=(n_tiles + 1,),
        in_specs=[
            pl.BlockSpec((tm, D_MODEL), lambda s: (tile(s)[0], 0)),
            pl.BlockSpec((1, D_MODEL), lambda s: (0, 0)),
            pl.BlockSpec((1, D_MODEL), lambda s: (0, 0)),
            pl.BlockSpec((D_MODEL, tn), lambda s: (0, tile(s)[1])),
            tab_spec, tab_spec, tab_spec,
        ],
        out_specs=[
            pl.BlockSpec((tm, D_MODEL), lambda s: (tile(s)[0], 0)),
            pl.BlockSpec((tm, tn), lambda s: prev_tile(s)),
        ],
        out_shape=[
            jax.ShapeDtypeStruct((t, D_MODEL), F32),
            jax.ShapeDtypeStruct((t, IN_WIDTH), BF16),
        ],
        scratch_shapes=[pltpu.VMEM((tm, D_MODEL), BF16), pltpu.VMEM((tm, tn), F32)],
        compiler_params=_cparams("arbitrary"),
        name="inproj",
    )(x2, g, b, w16, *tabs)


DIL_BQ = 128
DIL_WIN = 256
DIL_GROUP = 8
DIL_BASE = 4


def _dil_plan(seq, window, d):
    length = seq // d
    half = window // (2 * d)
    bq = min(DIL_BQ, length)
    win = min(DIL_WIN, length)
    nblk = length // bq
    blocks = min(nblk, DIL_GROUP)
    residues = max(1, min(d, DIL_GROUP // nblk))
    assert seq % d == 0 and length % bq == 0 and nblk % blocks == 0 and d % residues == 0
    assert win >= min(length, bq + 2 * half) and half % BF16_ROWS == 0
    return length, half, bq, win, nblk, blocks, residues


def _dil_attn_kernel(q_ref, k_ref, v_ref, g_ref, o_ref,
                     stage, q4, k4, v4, qd, kd, vd, vaug, acc, m_s, l_s, *, seq):
    scale = HEAD_DIM ** -0.5
    quarter = seq // DIL_BASE
    for src, dst in ((q_ref, q4), (k_ref, k4), (v_ref, v4)):
        stage[...] = src[...].astype(F32)
        for r in range(DIL_BASE):
            dst[pl.ds(r * quarter, quarter), :] = stage[pl.ds(r, quarter, stride=DIL_BASE), :]
    vaug[:, :LANES] = v_ref[...]
    vaug[:, LANES:] = jnp.ones((seq, LANES), BF16)
    vd[:, LANES:] = jnp.ones((vd.shape[0], LANES), BF16)

    def run_config(first, window, d):
        length, half, bq, win, nblk, blocks, residues = _dil_plan(seq, window, d)

        def group(rg, bg):
            if d == 1:
                qs, ks, vs = q_ref, k_ref, vaug
            else:
                qs, ks, vs = qd, kd, vd
            items = []
            for j in range(residues):
                r = rg * residues + j
                for u in range(blocks):
                    q0 = pl.multiple_of((bg * blocks + u) * bq, bq)
                    k0 = pl.multiple_of(jnp.clip(q0 - half, 0, length - win), BF16_ROWS)
                    rows = pl.ds(q0, bq) if d == 1 else pl.ds(r + d * q0, bq, stride=d)
                    items.append((j * length, q0, k0, rows))
            scores = []
            for base, q0, k0, _ in items:
                s = _dot_nt(qs[pl.ds(base + q0, bq), :], ks[pl.ds(base + k0, win), :]) * scale
                qpos = q0 + lax.broadcasted_iota(jnp.int32, (bq, win), 0)
                kpos = k0 + lax.broadcasted_iota(jnp.int32, (bq, win), 1)
                scores.append(jnp.where(jnp.abs(qpos - kpos) <= half, s, NEG_BIG))
            m_new = [jnp.broadcast_to(jnp.max(s, -1, keepdims=True), (bq, LANES)) for s in scores]
            if not first:
                m_old = [m_s[rows, :] for _, _, _, rows in items]
                m_new = [jnp.maximum(a, b) for a, b in zip(m_old, m_new)]
            def widen(m):
                return jnp.tile(m, (1, win // LANES)) if win % LANES == 0 else m[:, :1]
            probs = [jnp.exp(s - widen(m)).astype(BF16) for s, m in zip(scores, m_new)]
            pv = [jnp.dot(p, vs[pl.ds(base + k0, win), :], preferred_element_type=F32)
                  for p, (base, _, k0, _) in zip(probs, items)]
            for idx, (_, _, _, rows) in enumerate(items):
                a_new, l_new = pv[idx][:, :LANES], pv[idx][:, LANES:]
                if not first:
                    alpha = jnp.exp(m_old[idx] - m_new[idx])
                    a_new = alpha * acc[rows, :] + a_new
                    l_new = alpha * l_s[rows, :] + l_new
                acc[rows, :] = a_new
                m_s[rows, :] = m_new[idx]
                l_s[rows, :] = l_new

        def residue_group(rg, carry):
            if d > 1:
                for j in range(residues):
                    r = rg * residues + j
                    sub = pl.ds((r % DIL_BASE) * quarter + r // DIL_BASE, length, stride=d // DIL_BASE)
                    dst = pl.ds(j * length, length)
                    qd[dst, :] = q4[sub, :].astype(BF16)
                    kd[dst, :] = k4[sub, :].astype(BF16)
                    vd[dst, :LANES] = v4[sub, :].astype(BF16)
            if nblk == blocks:
                group(rg, 0)
            else:
                lax.fori_loop(0, nblk // blocks, lambda bg, c: (group(rg, bg), c)[1], 0)
            return carry

        if d == residues:
            residue_group(0, 0)
        else:
            lax.fori_loop(0, d // residues, residue_group, 0)

    for ci, (window, d) in enumerate(sorted(DIL_CONFIGS, key=lambda wd: -wd[1])):
        run_config(ci == 0, window, d)

    o_ref[...] = _head_rms(acc[...] / l_s[...], g_ref[...]).astype(o_ref.dtype)


def _dil_attn(proj3, g):
    bsz, seq, _ = proj3.shape
    assert all(d == 1 or d % DIL_BASE == 0 for _, d in DIL_CONFIGS) and seq % (DIL_BASE * SUBLANES) == 0
    plans = [_dil_plan(seq, window, d) for window, d in DIL_CONFIGS if d > 1]
    sub_rows = max(length * residues for length, _, _, _, _, _, residues in plans)

    def spec(col0):
        return pl.BlockSpec((None, seq, LANES), lambda b, h: (b, 0, col0 + h))

    return pl.pallas_call(
        functools.partial(_dil_attn_kernel, seq=seq),
        grid=(bsz, N_DIL_HEADS),
        in_specs=[spec(COL_QA), spec(COL_KA), spec(COL_VA), pl.BlockSpec((1, LANES), lambda b, h: (0, 0))],
        out_specs=pl.BlockSpec((None, seq, LANES), lambda b, h: (b, 0, h)),
        out_shape=jax.ShapeDtypeStruct((bsz, seq, DIL_WIDTH), BF16),
        scratch_shapes=[pltpu.VMEM((seq, LANES), F32)] * 4
        + [pltpu.VMEM((sub_rows, LANES), BF16)] * 2
        + [pltpu.VMEM((sub_rows, 2 * LANES), BF16), pltpu.VMEM((seq, 2 * LANES), BF16)]
        + [pltpu.VMEM((seq, LANES), F32)] * 3,
        compiler_params=_cparams("parallel", "parallel"),
        name="dil_attn",
    )(proj3, proj3, proj3, g)


DIFF_TQ = 512
DIFF_KEY_CHUNK = 1024


def _diff_attn_kernel(q_ref, k_ref, v_ref, lq1, lk1, lq2, lk2, g_ref, o_ref, vaug, s_scr, mx_scr, *, lam_init):
    seq = k_ref.shape[0]
    kc = min(DIFF_KEY_CHUNK, seq)
    assert seq % kc == 0
    chunks = [pl.ds(c * kc, kc) for c in range(seq // kc)]

    @pl.when(pl.program_id(2) == 0)
    def _():
        vaug[:, :LANES] = v_ref[...]
        vaug[:, LANES:] = jnp.ones((seq, LANES), BF16)
        s_scr[...] = jnp.zeros_like(s_scr)
        mx_scr[...] = jnp.zeros_like(mx_scr)

    scale = DIFF_QK_DIM ** -0.5
    assert 2.0 ** round(math.log2(scale)) == scale
    q = q_ref[...] * scale
    lo = lax.broadcasted_iota(jnp.int32, q.shape, 1) < DIFF_QK_DIM
    zero = jnp.zeros_like(q)
    q_maps = (jnp.where(lo, q, zero), jnp.where(lo, zero, q))

    outs = []
    for mp in range(2):
        m_prev = mx_scr[mp][:, :1]
        pv, m_next = None, None
        for c in chunks:
            p = jnp.exp(s_scr[mp, :, c] - m_prev).astype(BF16)
            part = jnp.dot(p, vaug[c, :], preferred_element_type=F32)
            pv = part if pv is None else pv + part
            s_new = _dot_nt(q_maps[mp], k_ref[c, :])
            s_scr[mp, :, c] = s_new
            m_chunk = jnp.max(s_new, -1, keepdims=True)
            m_next = m_chunk if m_next is None else jnp.maximum(m_next, m_chunk)
        mx_scr[mp] = jnp.broadcast_to(m_next, mx_scr.shape[1:])
        outs.append(pv[:, :LANES] / pv[:, LANES:])
    o0, o1 = outs
    lam = (jnp.exp(jnp.sum(lq1[...] * lk1[...], keepdims=True))
           - jnp.exp(jnp.sum(lq2[...] * lk2[...], keepdims=True)) + lam_init)
    a = o0 - lam * o1
    o_ref[...] = (_head_rms(a, g_ref[...]) * (1.0 - lam_init)).astype(o_ref.dtype)


def _diff_attn(proj3, lq1, lk1, lq2, lk2, g, lam_init):
    bsz, seq, _ = proj3.shape
    tq = DIFF_TQ
    assert seq % tq == 0
    n_tiles = seq // tq
    vec = pl.BlockSpec((1, DIFF_QK_DIM), lambda b, h, i: (0, 0))
    return pl.pallas_call(
        functools.partial(_diff_attn_kernel, lam_init=lam_init),
        grid=(bsz, N_DIFF_HEADS, n_tiles + 1),
        in_specs=[
            pl.BlockSpec((None, tq, LANES), lambda b, h, i: (b, jnp.minimum(i, n_tiles - 1), COL_QD + h)),
            pl.BlockSpec((None, seq, LANES), lambda b, h, i: (b, 0, COL_KD + h)),
            pl.BlockSpec((None, seq, LANES), lambda b, h, i: (b, 0, COL_VD + h)),
            vec, vec, vec, vec,
            pl.BlockSpec((1, DIFF_V_DIM), lambda b, h, i: (0, 0)),
        ],
        out_specs=pl.BlockSpec((None, tq, LANES), lambda b, h, i: (b, jnp.maximum(i - 1, 0), h)),
        out_shape=jax.ShapeDtypeStruct((bsz, seq, DIFF_WIDTH), BF16),
        scratch_shapes=[pltpu.VMEM((seq, 2 * LANES), BF16), pltpu.VMEM((2, tq, seq), F32),
                        pltpu.VMEM((2, tq, LANES), F32)],
        compiler_params=_cparams("parallel", "parallel", "arbitrary"),
        name="diff_attn",
    )(proj3, proj3, proj3, lq1, lk1, lq2, lk2, g)


OUTPROJ_TM = 256


def _outproj_kernel(ma_ref, md_ref, wa_ref, wd_ref, h_ref, g_ref, b_ref, o32_ref, o16_ref, o16t_ref):
    mix = jnp.dot(ma_ref[...], wa_ref[...], preferred_element_type=F32)
    mix = mix + jnp.dot(md_ref[...], wd_ref[...], preferred_element_type=F32)
    h = _layer_norm(DEEPNORM_ALPHA * h_ref[...] + mix, g_ref[...], b_ref[...])
    o32_ref[...] = h
    o16_ref[...] = h.astype(BF16)
    o16t_ref[...] = h.T.astype(BF16)


def _outproj(mix_a, mix_d, w_a, w_d, h0, g, b):
    t = h0.shape[0]
    tm = OUTPROJ_TM
    assert t % tm == 0
    row = pl.BlockSpec((1, D_MODEL), lambda i: (0, 0))
    return pl.pallas_call(
        _outproj_kernel,
        grid=(t // tm,),
        in_specs=[
            pl.BlockSpec((tm, DIL_WIDTH), lambda i: (i, 0)),
            pl.BlockSpec((tm, DIFF_WIDTH), lambda i: (i, 0)),
            pl.BlockSpec((DIL_WIDTH, D_MODEL), lambda i: (0, 0)),
            pl.BlockSpec((DIFF_WIDTH, D_MODEL), lambda i: (0, 0)),
            pl.BlockSpec((tm, D_MODEL), lambda i: (i, 0)),
            row, row,
        ],
        out_specs=[pl.BlockSpec((tm, D_MODEL), lambda i: (i, 0))] * 2
        + [pl.BlockSpec((D_MODEL, tm), lambda i: (0, i))],
        out_shape=[jax.ShapeDtypeStruct((t, D_MODEL), F32), jax.ShapeDtypeStruct((t, D_MODEL), BF16),
                   jax.ShapeDtypeStruct((D_MODEL, t), BF16)],
        compiler_params=_cparams("parallel"),
        name="outproj",
    )(mix_a, mix_d, w_a, w_d, h0, g, b)


SCORE_TM = 512
N_KEYSETS = 2 * PEER_HEADS
HALF_KEY_DIM = PEER_KEY_DIM // 2


def _peer_score_kernel(h_ref, wq_ref, keys_ref, s_ref):
    q = jnp.dot(h_ref[...], wq_ref[...], preferred_element_type=F32).astype(BF16)
    for hc in range(N_KEYSETS):
        qs = q[:, hc * HALF_KEY_DIM:(hc + 1) * HALF_KEY_DIM]
        s_ref[hc] = _dot_nt(keys_ref[hc], qs)


def _peer_scores(h16, wq16, keys16):
    t = h16.shape[0]
    tm = SCORE_TM
    assert t % tm == 0 and HALF_KEY_DIM == LANES
    return pl.pallas_call(
        _peer_score_kernel,
        grid=(t // tm,),
        in_specs=[
            pl.BlockSpec((tm, D_MODEL), lambda i: (i, 0)),
            pl.BlockSpec((D_MODEL, PEER_HEADS * PEER_KEY_DIM), lambda i: (0, 0)),
            pl.BlockSpec((N_KEYSETS, PEER_N_KEYS, HALF_KEY_DIM), lambda i: (0, 0, 0)),
        ],
        out_specs=pl.BlockSpec((N_KEYSETS, PEER_N_KEYS, tm), lambda i: (0, 0, i)),
        out_shape=jax.ShapeDtypeStruct((N_KEYSETS, PEER_N_KEYS, t), F32),
        compiler_params=_cparams("parallel"),
        name="peer_scores",
    )(h16, wq16, keys16)


ROUTE_TT = 1024


SUBLANES = 8


def _sort_network(n):
    def merge(lo, hi, r):
        step = r * 2
        if step < hi - lo:
            yield from merge(lo, hi, step)
            yield from merge(lo + r, hi, step)
            yield from ((i, i + r) for i in range(lo + r, hi - r, step))
        else:
            yield (lo, lo + r)

    def sort(lo, hi):
        if hi - lo >= 1:
            mid = lo + (hi - lo) // 2
            yield from sort(lo, mid)
            yield from sort(mid + 1, hi)
            yield from merge(lo, hi, 1)

    return list(sort(0, n - 1))


def _bitonic_merge_network(n):
    pairs, j = [], n // 2
    while j >= 1:
        pairs += [(i, i ^ j) for i in range(n) if i ^ j > i]
        j //= 2
    return pairs


def _compare_exchange_desc(xs, pairs):
    xs = list(xs)
    for lo, hi in pairs:
        xs[lo], xs[hi] = jnp.maximum(xs[lo], xs[hi]), jnp.minimum(xs[lo], xs[hi])
    return xs


def _top_sorted(tiles):
    k = len(tiles)
    top = _compare_exchange_desc(tiles, _sort_network(k))
    shift = 1
    while shift < SUBLANES:
        other = [pltpu.roll(x, shift, 0) for x in top]
        top = [jnp.maximum(top[i], other[k - 1 - i]) for i in range(k)]
        top = _compare_exchange_desc(top, _bitonic_merge_network(k))
        shift *= 2
    return top


def _count_greater(x, top):
    assert len(top) == 16
    b3 = top[7] > x
    b2 = jnp.where(b3, top[11], top[3]) > x
    b1 = jnp.where(b3, jnp.where(b2, top[13], top[9]), jnp.where(b2, top[5], top[1])) > x
    q = [jnp.where(b1, top[4 * i + 2], top[4 * i]) for i in range(4)]
    b0 = jnp.where(b3, jnp.where(b2, q[3], q[2]), jnp.where(b2, q[1], q[0])) > x
    count = (jnp.where(b3, 8.0, 0.0) + jnp.where(b2, 4.0, 0.0)) + (jnp.where(b1, 2.0, 0.0) + jnp.where(b0, 1.0, 0.0))
    return jnp.where(top[15] > x, 16.0, count)


def _top_ranks(v, k):
    n = v.shape[0]
    rows = lax.broadcasted_iota(jnp.int32, v.shape, 0)
    rank = jnp.full(v.shape, k, jnp.int32)
    work = v
    vals = []
    for r in range(k):
        m = jnp.max(work, axis=0, keepdims=True)
        first = jnp.min(jnp.where(work == m, rows, n), axis=0, keepdims=True)
        sel = rows == first
        rank = jnp.where(sel, r, rank)
        work = jnp.where(sel, -jnp.inf, work)
        vals.append(m)
    return jnp.concatenate(vals, axis=0), rank


def _staircase(sv0, sv1, k):
    arow = lax.broadcasted_iota(jnp.int32, sv0.shape, 0)
    count = jnp.zeros(sv0.shape, jnp.int32)
    front = sv0 + sv1[0:1]
    best = front[0:1]
    z = jnp.zeros_like(best)
    for _ in range(k):
        m = jnp.max(front, axis=0, keepdims=True)
        first = jnp.min(jnp.where(front == m, arow, k), axis=0, keepdims=True)
        win = arow == first
        z = z + jnp.exp(m - best)
        count = count + win.astype(jnp.int32)
        nw = jnp.sum(jnp.where(win, count, 0), axis=0, keepdims=True)
        nxt = jnp.sum(jnp.where(arow == nw, sv1, 0.0), axis=0, keepdims=True)
        nxt = jnp.where(nw < k, nxt, -jnp.inf)
        front = jnp.where(win, sv0 + nxt, front)
    return count.astype(F32), z


def _peer_route_kernel(s_ref, n_ref, e0_ref, r1_ref, e1_ref, sv0_s, rank0_s, sv1_s, rank1_s, cnt_s, zinv_s):
    k = PEER_TOPK
    n_keys, tt = s_ref.shape[1], s_ref.shape[2]
    n_tiles = n_keys // SUBLANES
    assert n_tiles == k
    sides = ((sv0_s, rank0_s), (sv1_s, rank1_s))

    def lane_col(c):
        return pl.ds(pl.multiple_of(c * LANES, LANES), LANES)

    def tiles_of(side, col):
        return [s_ref[side, g * SUBLANES:(g + 1) * SUBLANES, col] for g in range(n_tiles)]

    def sort_col(c, tied):
        col = lane_col(c)
        for side, (sv_s, _) in enumerate(sides):
            tiles = tiles_of(side, col)
            top = _top_sorted(tiles)
            for r in range(k):
                sv_s[r:r + 1, col] = top[r][0:1]
            equal_pair = functools.reduce(jnp.logical_or, [top[r] == top[r + 1] for r in range(k - 1)])
            at_least = sum(jnp.where(x >= top[k - 1], 1.0, 0.0) for x in tiles)
            bad = jnp.where(equal_pair, 1.0, 0.0) + jnp.abs(jnp.sum(at_least, axis=0, keepdims=True) - k)
            tied = jnp.maximum(tied, jnp.max(bad))
            if side == 1:
                for g, x in enumerate(tiles):
                    r1_ref[g * SUBLANES:(g + 1) * SUBLANES, col] = _count_greater(x, top).astype(r1_ref.dtype)
        return tied

    tied = lax.fori_loop(0, tt // LANES, sort_col, jnp.zeros((), F32)) > 0

    @pl.when(tied)
    def _():
        for side, (sv_s, rank_s) in enumerate(sides):
            sv, rank = _top_ranks(s_ref[side], k)
            sv_s[...] = sv
            rank_s[...] = rank
        r1_ref[...] = rank1_s[...].astype(F32).astype(r1_ref.dtype)

    count, z = _staircase(sv0_s[...], sv1_s[...], k)
    cnt_s[...] = count
    zinv_s[...] = 1.0 / z

    def emit_col(c, carry):
        col = lane_col(c)
        s0 = s_ref[0, :, col]
        e0_ref[:, col] = jnp.exp(s0 - sv0_s[0:1, col])
        e1_ref[:, col] = (jnp.exp(s_ref[1, :, col] - sv1_s[0:1, col]) * zinv_s[:, col]).astype(e1_ref.dtype)
        n_of_row = jnp.zeros(s0.shape, F32)
        for a in range(k):
            n_of_row = jnp.where(s0 == sv0_s[a:a + 1, col], cnt_s[a:a + 1, col], n_of_row)
        n_ref[:, col] = n_of_row
        return carry

    lax.fori_loop(0, tt // LANES, emit_col, 0)

    @pl.when(tied)
    def _():
        rank0 = rank0_s[...]
        cnt = cnt_s[...]
        n_of_row = jnp.zeros(rank0.shape, F32)
        for a in range(k):
            n_of_row = jnp.where(rank0 == a, cnt[a:a + 1], n_of_row)
        n_ref[...] = n_of_row


def _peer_route(scores):
    t = scores.shape[-1]
    tt = ROUTE_TT
    assert t % tt == 0 and tt % LANES == 0
    out_spec = pl.BlockSpec((None, PEER_N_KEYS, tt), lambda h, i: (h, 0, i))
    shape = (PEER_HEADS, PEER_N_KEYS, t)
    return pl.pallas_call(
        _peer_route_kernel,
        grid=(PEER_HEADS, t // tt),
        in_specs=[pl.BlockSpec((2, PEER_N_KEYS, tt), lambda h, i: (h, 0, i))],
        out_specs=[out_spec] * 4,
        out_shape=[jax.ShapeDtypeStruct(shape, F32), jax.ShapeDtypeStruct(shape, F32),
                   jax.ShapeDtypeStruct(shape, BF16), jax.ShapeDtypeStruct(shape, BF16)],
        scratch_shapes=[pltpu.VMEM((PEER_TOPK, tt), F32), pltpu.VMEM((PEER_N_KEYS, tt), jnp.int32),
                        pltpu.VMEM((PEER_TOPK, tt), F32), pltpu.VMEM((PEER_N_KEYS, tt), jnp.int32),
                        pltpu.VMEM((PEER_TOPK, tt), F32), pltpu.VMEM((1, tt), F32)],
        compiler_params=_cparams("parallel", "parallel"),
        name="peer_route",
    )(scores)


EXPERT_TM = 512
EXPERT_TE = 512
EXPERT_DRAIN_STEPS = 2
INV_SQRT2 = 1.0 / math.sqrt(2.0)


def _expert_work(s, lag, n_work):
    return jnp.clip(s - lag, 0, n_work - 1)


def _peer_expert_kernel(xt_ref, u_ref, vt_ref, n_ref, e0_ref, r1_ref, e1_ref, h_ref, g_ref, b_ref,
                        o_ref, acc_ref, pre_ref, hid_ref, *, n_blocks, n_work):
    s = pl.program_id(0)
    te = u_ref.shape[0]
    tm = xt_ref.shape[1]
    rows_per_block = te // PEER_N_KEYS
    zero = jnp.zeros((), BF16)
    block_down = _expert_work(s, 2, n_work) % n_blocks
    block_gate = _expert_work(s, 1, n_work) % n_blocks

    @pl.when(s == 0)
    def _():
        pre_ref[...] = jnp.zeros_like(pre_ref)
        hid_ref[...] = jnp.zeros_like(hid_ref)

    @pl.when(block_down == 0)
    def _():
        acc_ref[...] = jnp.zeros_like(acc_ref)

    acc_ref[...] += jnp.dot(vt_ref[...], hid_ref[...], preferred_element_type=F32)

    a = pre_ref[...]
    act = (0.5 * a * (1.0 + lax.erf(a * INV_SQRT2))).astype(BF16)
    first_row = block_gate * rows_per_block
    gates = []
    for ib in range(rows_per_block):
        i = first_row + ib
        w = jnp.zeros((PEER_N_KEYS, tm), BF16)
        for hd in range(PEER_HEADS):
            n_b = jnp.broadcast_to(n_ref[hd, pl.ds(i, 1), :], (PEER_N_KEYS, tm)).astype(BF16)
            e0_b = jnp.broadcast_to(e0_ref[hd, pl.ds(i, 1), :], (PEER_N_KEYS, tm)).astype(BF16)
            w = w + jnp.where(r1_ref[hd] < n_b, e1_ref[hd], zero) * e0_b
        gates.append(w)
    hid_ref[...] = jnp.concatenate(gates, axis=0) * act

    pre_ref[...] = jnp.dot(u_ref[...], xt_ref[...], preferred_element_type=F32)

    @pl.when(jnp.logical_and(block_down == n_blocks - 1, s >= EXPERT_DRAIN_STEPS))
    def _():
        y = DEEPNORM_ALPHA * h_ref[...] + acc_ref[...].T
        o_ref[...] = _layer_norm(y, g_ref[...], b_ref[...])


def _peer_experts(h16t, u16, v16t, n_t, e0_t, r1_t, e1_t, h32, g, b):
    t = h16t.shape[1]
    tm, te = EXPERT_TM, EXPERT_TE
    assert t % tm == 0 and PEER_N_EXPERTS % te == 0 and te % PEER_N_KEYS == 0
    n_blocks = PEER_N_EXPERTS // te
    n_work = (t // tm) * n_blocks

    def item(lag):
        return lambda s: divmod(_expert_work(s, lag, n_work), n_blocks)

    up, gate, down = item(0), item(1), item(2)
    route = pl.BlockSpec((PEER_HEADS, PEER_N_KEYS, tm), lambda s: (0, 0, gate(s)[0]))
    row = pl.BlockSpec((1, D_MODEL), lambda s: (0, 0))
    return pl.pallas_call(
        functools.partial(_peer_expert_kernel, n_blocks=n_blocks, n_work=n_work),
        grid=(n_work + EXPERT_DRAIN_STEPS,),
        in_specs=[
            pl.BlockSpec((D_MODEL, tm), lambda s: (0, up(s)[0])),
            pl.BlockSpec((te, D_MODEL), lambda s: (up(s)[1], 0)),
            pl.BlockSpec((None, D_MODEL, te), lambda s: (down(s)[1], 0, 0)),
            route, route, route, route,
            pl.BlockSpec((tm, D_MODEL), lambda s: (down(s)[0], 0)),
            row, row,
        ],
        out_specs=pl.BlockSpec((tm, D_MODEL), lambda s: (down(s)[0], 0)),
        out_shape=jax.ShapeDtypeStruct((t, D_MODEL), F32),
        scratch_shapes=[pltpu.VMEM((D_MODEL, tm), F32), pltpu.VMEM((te, tm), F32), pltpu.VMEM((te, tm), BF16)],
        compiler_params=_cparams("arbitrary"),
        name="peer_experts",
    )(h16t, u16, v16t, n_t, e0_t, r1_t, e1_t, h32, g, b)


def kernel(x, ln_emb_g, ln_emb_b, w_in, dil_norm_g, lambda_q1, lambda_k1, lambda_q2, lambda_k2, subln_g,
           w_out, ln1_g, ln1_b, peer_w_query, peer_sub_keys, peer_u, peer_v, ln2_g, ln2_b):
    bsz, seq, d_model = x.shape
    assert d_model == D_MODEL and w_in.shape[0] == DEPTH == 1
    t = bsz * seq
    row = lambda p: p.reshape(1, -1).astype(F32)
    lam_init = 0.8 - 0.6 * math.exp(-0.3 * 0)

    h0, proj = _inproj(x.reshape(t, D_MODEL), row(ln_emb_g), row(ln_emb_b), w_in[0].astype(BF16), seq)
    proj3 = proj.reshape(bsz, seq, IN_WIDTH)

    mix_a = _dil_attn(proj3, row(dil_norm_g[0])).reshape(t, DIL_WIDTH)
    mix_d = _diff_attn(proj3, row(lambda_q1[0]), row(lambda_k1[0]), row(lambda_q2[0]), row(lambda_k2[0]),
                       row(subln_g[0]), lam_init).reshape(t, DIFF_WIDTH)

    w_out16 = w_out[0].astype(BF16)
    h1, h1_16, h1_16t = _outproj(mix_a, mix_d, w_out16[:DIL_WIDTH], w_out16[DIL_WIDTH:], h0,
                                 row(ln1_g[0]), row(ln1_b[0]))

    keys16 = peer_sub_keys[0].reshape(N_KEYSETS, PEER_N_KEYS, HALF_KEY_DIM).astype(BF16)
    scores = _peer_scores(h1_16, peer_w_query[0].astype(BF16), keys16)
    n_t, e0_t, r1_t, e1_t = _peer_route(scores)
    v16t = peer_v[0].astype(BF16).reshape(PEER_N_EXPERTS // EXPERT_TE, EXPERT_TE, D_MODEL).transpose(0, 2, 1)
    out = _peer_experts(h1_16t, peer_u[0].astype(BF16), v16t, n_t, e0_t, r1_t, e1_t,
                        h1, row(ln2_g[0]), row(ln2_b[0]))
    return out.reshape(bsz, seq, D_MODEL)
```

```python
import functools
import math

import jax
import jax.numpy as jnp
from jax import lax
from jax.experimental import pallas as pl
from jax.experimental.pallas import tpu as pltpu

F32 = jnp.float32
BF16 = jnp.bfloat16

D_MODEL = 2048
DEPTH = 1
HEAD_DIM = 128
DIFF_WIDTH = D_MODEL // 4
DIL_WIDTH = D_MODEL - DIFF_WIDTH
N_DIL_HEADS = DIL_WIDTH // HEAD_DIM
DIL_CONFIGS = ((128, 1), (512, 4), (2048, 16))
DIFF_QK_DIM = 64
DIFF_V_DIM = 2 * DIFF_QK_DIM
N_DIFF_HEADS = DIFF_WIDTH // DIFF_V_DIM
DIFF_QK_WIDTH = N_DIFF_HEADS * 2 * DIFF_QK_DIM
IN_WIDTH = 3 * DIL_WIDTH + 2 * DIFF_QK_WIDTH + DIFF_WIDTH
ROPE_THETA = 500000.0
ROPE_FRACTION = 4
PEER_HEADS = 8
PEER_N_KEYS = 128
PEER_N_EXPERTS = PEER_N_KEYS * PEER_N_KEYS
PEER_KEY_DIM = 256
PEER_TOPK = 16
LN_EPS = 1e-5
NEG_BIG = -1e30
DEEPNORM_ALPHA = (2.0 * DEPTH) ** 0.25

LANES = 128
BF16_ROWS = 16
VMEM_LIMIT = 56 * 1024 * 1024

COL_QA = 0
COL_KA = DIL_WIDTH // LANES
COL_VA = 2 * DIL_WIDTH // LANES
COL_QD = 3 * DIL_WIDTH // LANES
COL_KD = COL_QD + DIFF_QK_WIDTH // LANES
COL_VD = COL_KD + DIFF_QK_WIDTH // LANES


def _cparams(*sem, vmem_limit=VMEM_LIMIT):
    return pltpu.CompilerParams(dimension_semantics=sem, vmem_limit_bytes=vmem_limit)


def _layer_norm(x, g, b):
    mu = jnp.mean(x, -1, keepdims=True)
    xc = x - mu
    var = jnp.mean(xc * xc, -1, keepdims=True)
    return xc * lax.rsqrt(var + LN_EPS) * g + b


def _head_rms(t, g):
    return t * lax.rsqrt(jnp.mean(t * t, -1, keepdims=True) + LN_EPS) * g


def _dot_nt(a, b):
    return lax.dot_general(a, b, (((1,), (1,)), ((), ())), preferred_element_type=F32)


INPROJ_TM = 1024
INPROJ_TN = 512
ROPE_DIL_SHIFT = HEAD_DIM // ROPE_FRACTION // 2
ROPE_DIFF_SHIFT = DIFF_QK_DIM // ROPE_FRACTION // 2


def _rope_tables(seq, period, half):
    rot = 2 * half
    inv_freq = 1.0 / (ROPE_THETA ** (jnp.arange(0, rot, 2, dtype=F32) / rot))
    ang = jnp.arange(seq, dtype=F32)[:, None] * inv_freq[None, :]
    cos, sin = jnp.cos(ang), jnp.sin(ang)
    pad = period - rot
    c = jnp.concatenate([cos, cos, jnp.ones((seq, pad), F32)], axis=1)
    a = jnp.concatenate([-sin, jnp.zeros((seq, half + pad), F32)], axis=1)
    b = jnp.concatenate([jnp.zeros((seq, half), F32), sin, jnp.zeros((seq, pad), F32)], axis=1)
    reps = LANES // period
    return tuple(jnp.tile(t, (1, reps)) for t in (c, a, b))


ROPE_KIND_DIL, ROPE_KIND_DIFF, ROPE_KIND_NONE = 0, 1, 2


def _inproj_kernel(x_ref, g_ref, b_ref, w_ref, c_ref, a_ref, bt_ref, h_ref, o_ref, xs_ref, raw_ref,
                   *, n_col_blocks, kind_of_block):
    s = pl.program_id(0)
    last_tile = pl.num_programs(0) - 2
    j = jnp.minimum(s, last_tile) % n_col_blocks
    j_prev = jnp.maximum(s - 1, 0) % n_col_blocks

    @pl.when(s == 0)
    def _():
        raw_ref[...] = jnp.zeros_like(raw_ref)

    @pl.when(jnp.logical_and(j == 0, s <= last_tile))
    def _():
        h = _layer_norm(x_ref[...], g_ref[...], b_ref[...])
        h_ref[...] = h
        xs_ref[...] = h.astype(BF16)

    shift = jnp.where(kind_of_block(j_prev) == ROPE_KIND_DIFF, ROPE_DIFF_SHIFT, ROPE_DIL_SHIFT)
    c, a, b = c_ref[...], a_ref[...], bt_ref[...]
    for gidx in range(raw_ref.shape[1] // LANES):
        t = raw_ref[:, gidx * LANES:(gidx + 1) * LANES]
        r = t * c + pltpu.roll(t, LANES - shift, 1) * a + pltpu.roll(t, shift, 1) * b
        o_ref[:, gidx * LANES:(gidx + 1) * LANES] = r.astype(o_ref.dtype)

    raw_ref[...] = jnp.dot(xs_ref[...], w_ref[...], preferred_element_type=F32)


def _inproj(x2, g, b, w16, seq):
    t = x2.shape[0]
    tm, tn = INPROJ_TM, INPROJ_TN
    assert t % tm == 0 and seq % tm == 0 and IN_WIDTH % tn == 0
    assert (2 * DIL_WIDTH) % tn == 0 and (3 * DIL_WIDTH) % tn == 0 and DIFF_QK_WIDTH * 2 % tn == 0
    sblocks = seq // tm
    n_col = IN_WIDTH // tn
    n_tiles = (t // tm) * n_col
    n_dil, diff_lo = 2 * DIL_WIDTH // tn, 3 * DIL_WIDTH // tn
    diff_hi = (3 * DIL_WIDTH + 2 * DIFF_QK_WIDTH) // tn

    def kind_of_block(j):
        return jnp.where(j < n_dil, ROPE_KIND_DIL,
                         jnp.where(jnp.logical_and(j >= diff_lo, j < diff_hi), ROPE_KIND_DIFF, ROPE_KIND_NONE))

    def tile(s):
        s = jnp.minimum(s, n_tiles - 1)
        return s // n_col, s % n_col

    def prev_tile(s):
        return tile(jnp.maximum(s - 1, 0))

    ident = (jnp.ones((seq, LANES), F32), jnp.zeros((seq, LANES), F32), jnp.zeros((seq, LANES), F32))
    tabs = [jnp.stack(per_kind) for per_kind in zip(_rope_tables(seq, HEAD_DIM, ROPE_DIL_SHIFT),
                                                     _rope_tables(seq, DIFF_QK_DIM, ROPE_DIFF_SHIFT), ident)]
    tab_spec = pl.BlockSpec(
        (None, tm, LANES), lambda s: (kind_of_block(prev_tile(s)[1]), prev_tile(s)[0] % sblocks, 0))
    return pl.pallas_call(
        functools.partial(_inproj_kernel, n_col_blocks=n_col, kind_of_block=kind_of_block),
        grid=(n_tiles + 1,),
        in_specs=[
            pl.BlockSpec((tm, D_MODEL), lambda s: (tile(s)[0], 0)),
            pl.BlockSpec((1, D_MODEL), lambda s: (0, 0)),
            pl.BlockSpec((1, D_MODEL), lambda s: (0, 0)),
            pl.BlockSpec((D_MODEL, tn), lambda s: (0, tile(s)[1])),
            tab_spec, tab_spec, tab_spec,
        ],
        out_specs=[
            pl.BlockSpec((tm, D_MODEL), lambda s: (tile(s)[0], 0)),
            pl.BlockSpec((tm, tn), lambda s: prev_tile(s)),
        ],
        out_shape=[
            jax.ShapeDtypeStruct((t, D_MODEL), F32),
            jax.ShapeDtypeStruct((t, IN_WIDTH), BF16),
        ],
        scratch_shapes=[pltpu.VMEM((tm, D_MODEL), BF16), pltpu.VMEM((tm, tn), F32)],
        compiler_params=_cparams("arbitrary"),
        name="inproj",
    )(x2, g, b, w16, *tabs)


DIL_BQ = 128
DIL_WIN = 256
DIL_GROUP = 8
DIL_BASE = 4


def _dil_plan(seq, window, d):
    length = seq // d
    half = window // (2 * d)
    bq = min(DIL_BQ, length)
    win = min(DIL_WIN, length)
    nblk = length // bq
    blocks = min(nblk, DIL_GROUP)
    residues = max(1, min(d, DIL_GROUP // nblk))
    assert seq % d == 0 and length % bq == 0 and nblk % blocks == 0 and d % residues == 0
    assert win >= min(length, bq + 2 * half) and half % BF16_ROWS == 0
    return length, half, bq, win, nblk, blocks, residues


def _dil_attn_kernel(q_ref, k_ref, v_ref, g_ref, o_ref,
                     stage, q4, k4, v4, qd, kd, vd, vaug, acc, m_s, l_s, *, seq):
    scale = HEAD_DIM ** -0.5
    quarter = seq // DIL_BASE
    for src, dst in ((q_ref, q4), (k_ref, k4), (v_ref, v4)):
        stage[...] = src[...].astype(F32)
        for r in range(DIL_BASE):
            dst[pl.ds(r * quarter, quarter), :] = stage[pl.ds(r, quarter, stride=DIL_BASE), :]
    vaug[:, :LANES] = v_ref[...]
    vaug[:, LANES:] = jnp.ones((seq, LANES), BF16)
    vd[:, LANES:] = jnp.ones((vd.shape[0], LANES), BF16)

    def run_config(first, window, d):
        length, half, bq, win, nblk, blocks, residues = _dil_plan(seq, window, d)

        def group(rg, bg):
            if d == 1:
                qs, ks, vs = q_ref, k_ref, vaug
            else:
                qs, ks, vs = qd, kd, vd
            items = []
            for j in range(residues):
                r = rg * residues + j
                for u in range(blocks):
                    q0 = pl.multiple_of((bg * blocks + u) * bq, bq)
                    k0 = pl.multiple_of(jnp.clip(q0 - half, 0, length - win), BF16_ROWS)
                    rows = pl.ds(q0, bq) if d == 1 else pl.ds(r + d * q0, bq, stride=d)
                    items.append((j * length, q0, k0, rows))
            scores = []
            for base, q0, k0, _ in items:
                s = _dot_nt(qs[pl.ds(base + q0, bq), :], ks[pl.ds(base + k0, win), :]) * scale
                qpos = q0 + lax.broadcasted_iota(jnp.int32, (bq, win), 0)
                kpos = k0 + lax.broadcasted_iota(jnp.int32, (bq, win), 1)
                scores.append(jnp.where(jnp.abs(qpos - kpos) <= half, s, NEG_BIG))
            m_new = [jnp.broadcast_to(jnp.max(s, -1, keepdims=True), (bq, LANES)) for s in scores]
            if not first:
                m_old = [m_s[rows, :] for _, _, _, rows in items]
                m_new = [jnp.maximum(a, b) for a, b in zip(m_old, m_new)]
            def widen(m):
                return jnp.tile(m, (1, win // LANES)) if win % LANES == 0 else m[:, :1]
            probs = [jnp.exp(s - widen(m)).astype(BF16) for s, m in zip(scores, m_new)]
            pv = [jnp.dot(p, vs[pl.ds(base + k0, win), :], preferred_element_type=F32)
                  for p, (base, _, k0, _) in zip(probs, items)]
            for idx, (_, _, _, rows) in enumerate(items):
                a_new, l_new = pv[idx][:, :LANES], pv[idx][:, LANES:]
                if not first:
                    alpha = jnp.exp(m_old[idx] - m_new[idx])
                    a_new = alpha * acc[rows, :] + a_new
                    l_new = alpha * l_s[rows, :] + l_new
                acc[rows, :] = a_new
                m_s[rows, :] = m_new[idx]
                l_s[rows, :] = l_new

        def residue_group(rg, carry):
            if d > 1:
                for j in range(residues):
                    r = rg * residues + j
                    sub = pl.ds((r % DIL_BASE) * quarter + r // DIL_BASE, length, stride=d // DIL_BASE)
                    dst = pl.ds(j * length, length)
                    qd[dst, :] = q4[sub, :].astype(BF16)
                    kd[dst, :] = k4[sub, :].astype(BF16)
                    vd[dst, :LANES] = v4[sub, :].astype(BF16)
            if nblk == blocks:
                group(rg, 0)
            else:
                lax.fori_loop(0, nblk // blocks, lambda bg, c: (group(rg, bg), c)[1], 0)
            return carry

        if d == residues:
            residue_group(0, 0)
        else:
            lax.fori_loop(0, d // residues, residue_group, 0)

    for ci, (window, d) in enumerate(sorted(DIL_CONFIGS, key=lambda wd: -wd[1])):
        run_config(ci == 0, window, d)

    o_ref[...] = _head_rms(acc[...] / l_s[...], g_ref[...]).astype(o_ref.dtype)


def _dil_attn(proj3, g):
    bsz, seq, _ = proj3.shape
    assert all(d == 1 or d % DIL_BASE == 0 for _, d in DIL_CONFIGS) and seq % (DIL_BASE * SUBLANES) == 0
    plans = [_dil_plan(seq, window, d) for window, d in DIL_CONFIGS if d > 1]
    sub_rows = max(length * residues for length, _, _, _, _, _, residues in plans)

    def spec(col0):
        return pl.BlockSpec((None, seq, LANES), lambda b, h: (b, 0, col0 + h))

    return pl.pallas_call(
        functools.partial(_dil_attn_kernel, seq=seq),
        grid=(bsz, N_DIL_HEADS),
        in_specs=[spec(COL_QA), spec(COL_KA), spec(COL_VA), pl.BlockSpec((1, LANES), lambda b, h: (0, 0))],
        out_specs=pl.BlockSpec((None, seq, LANES), lambda b, h: (b, 0, h)),
        out_shape=jax.ShapeDtypeStruct((bsz, seq, DIL_WIDTH), BF16),
        scratch_shapes=[pltpu.VMEM((seq, LANES), F32)] * 4
        + [pltpu.VMEM((sub_rows, LANES), BF16)] * 2
        + [pltpu.VMEM((sub_rows, 2 * LANES), BF16), pltpu.VMEM((seq, 2 * LANES), BF16)]
        + [pltpu.VMEM((seq, LANES), F32)] * 3,
        compiler_params=_cparams("parallel", "parallel"),
        name="dil_attn",
    )(proj3, proj3, proj3, g)


DIFF_TQ = 512
DIFF_KEY_CHUNK = 1024


def _diff_attn_kernel(q_ref, k_ref, v_ref, lq1, lk1, lq2, lk2, g_ref, o_ref, vaug, s_scr, mx_scr, *, lam_init):
    seq = k_ref.shape[0]
    kc = min(DIFF_KEY_CHUNK, seq)
    assert seq % kc == 0
    chunks = [pl.ds(c * kc, kc) for c in range(seq // kc)]

    @pl.when(pl.program_id(2) == 0)
    def _():
        vaug[:, :LANES] = v_ref[...]
        vaug[:, LANES:] = jnp.ones((seq, LANES), BF16)
        s_scr[...] = jnp.zeros_like(s_scr)
        mx_scr[...] = jnp.zeros_like(mx_scr)

    scale = DIFF_QK_DIM ** -0.5
    assert 2.0 ** round(math.log2(scale)) == scale
    q = q_ref[...] * scale
    lo = lax.broadcasted_iota(jnp.int32, q.shape, 1) < DIFF_QK_DIM
    zero = jnp.zeros_like(q)
    q_maps = (jnp.where(lo, q, zero), jnp.where(lo, zero, q))

    outs = []
    for mp in range(2):
        m_prev = mx_scr[mp][:, :1]
        pv, m_next = None, None
        for c in chunks:
            p = jnp.exp(s_scr[mp, :, c] - m_prev).astype(BF16)
            part = jnp.dot(p, vaug[c, :], preferred_element_type=F32)
            pv = part if pv is None else pv + part
            s_new = _dot_nt(q_maps[mp], k_ref[c, :])
            s_scr[mp, :, c] = s_new
            m_chunk = jnp.max(s_new, -1, keepdims=True)
            m_next = m_chunk if m_next is None else jnp.maximum(m_next, m_chunk)
        mx_scr[mp] = jnp.broadcast_to(m_next, mx_scr.shape[1:])
        outs.append(pv[:, :LANES] / pv[:, LANES:])
    o0, o1 = outs
    lam = (jnp.exp(jnp.sum(lq1[...] * lk1[...], keepdims=True))
           - jnp.exp(jnp.sum(lq2[...] * lk2[...], keepdims=True)) + lam_init)
    a = o0 - lam * o1
    o_ref[...] = (_head_rms(a, g_ref[...]) * (1.0 - lam_init)).astype(o_ref.dtype)


def _diff_attn(proj3, lq1, lk1, lq2, lk2, g, lam_init):
    bsz, seq, _ = proj3.shape
    tq = DIFF_TQ
    assert seq % tq == 0
    n_tiles = seq // tq
    vec = pl.BlockSpec((1, DIFF_QK_DIM), lambda b, h, i: (0, 0))
    return pl.pallas_call(
        functools.partial(_diff_attn_kernel, lam_init=lam_init),
        grid=(bsz, N_DIFF_HEADS, n_tiles + 1),
        in_specs=[
            pl.BlockSpec((None, tq, LANES), lambda b, h, i: (b, jnp.minimum(i, n_tiles - 1), COL_QD + h)),
            pl.BlockSpec((None, seq, LANES), lambda b, h, i: (b, 0, COL_KD + h)),
            pl.BlockSpec((None, seq, LANES), lambda b, h, i: (b, 0, COL_VD + h)),
            vec, vec, vec, vec,
            pl.BlockSpec((1, DIFF_V_DIM), lambda b, h, i: (0, 0)),
        ],
        out_specs=pl.BlockSpec((None, tq, LANES), lambda b, h, i: (b, jnp.maximum(i - 1, 0), h)),
        out_shape=jax.ShapeDtypeStruct((bsz, seq, DIFF_WIDTH), BF16),
        scratch_shapes=[pltpu.VMEM((seq, 2 * LANES), BF16), pltpu.VMEM((2, tq, seq), F32),
                        pltpu.VMEM((2, tq, LANES), F32)],
        compiler_params=_cparams("parallel", "parallel", "arbitrary"),
        name="diff_attn",
    )(proj3, proj3, proj3, lq1, lk1, lq2, lk2, g)


OUTPROJ_TM = 256


def _outproj_kernel(ma_ref, md_ref, wa_ref, wd_ref, h_ref, g_ref, b_ref, o32_ref, o16_ref, o16t_ref, raw_ref):
    @pl.when(pl.program_id(0) == 0)
    def _():
        raw_ref[...] = jnp.zeros_like(raw_ref)

    h = _layer_norm(DEEPNORM_ALPHA * h_ref[...] + raw_ref[...], g_ref[...], b_ref[...])
    o32_ref[...] = h
    o16_ref[...] = h.astype(BF16)
    o16t_ref[...] = h.T.astype(BF16)

    mix = jnp.dot(ma_ref[...], wa_ref[...], preferred_element_type=F32)
    raw_ref[...] = mix + jnp.dot(md_ref[...], wd_ref[...], preferred_element_type=F32)


def _outproj(mix_a, mix_d, w_a, w_d, h0, g, b):
    t = h0.shape[0]
    tm = OUTPROJ_TM
    assert t % tm == 0
    n_tiles = t // tm
    cur = lambda s: jnp.minimum(s, n_tiles - 1)
    prev = lambda s: jnp.maximum(s - 1, 0)
    row = pl.BlockSpec((1, D_MODEL), lambda s: (0, 0))
    return pl.pallas_call(
        _outproj_kernel,
        grid=(n_tiles + 1,),
        in_specs=[
            pl.BlockSpec((tm, DIL_WIDTH), lambda s: (cur(s), 0)),
            pl.BlockSpec((tm, DIFF_WIDTH), lambda s: (cur(s), 0)),
            pl.BlockSpec((DIL_WIDTH, D_MODEL), lambda s: (0, 0)),
            pl.BlockSpec((DIFF_WIDTH, D_MODEL), lambda s: (0, 0)),
            pl.BlockSpec((tm, D_MODEL), lambda s: (prev(s), 0)),
            row, row,
        ],
        out_specs=[pl.BlockSpec((tm, D_MODEL), lambda s: (prev(s), 0))] * 2
        + [pl.BlockSpec((D_MODEL, tm), lambda s: (0, prev(s)))],
        out_shape=[jax.ShapeDtypeStruct((t, D_MODEL), F32), jax.ShapeDtypeStruct((t, D_MODEL), BF16),
                   jax.ShapeDtypeStruct((D_MODEL, t), BF16)],
        scratch_shapes=[pltpu.VMEM((tm, D_MODEL), F32)],
        compiler_params=_cparams("arbitrary"),
        name="outproj",
    )(mix_a, mix_d, w_a, w_d, h0, g, b)


SCORE_TM = 512
N_KEYSETS = 2 * PEER_HEADS
HALF_KEY_DIM = PEER_KEY_DIM // 2


def _peer_score_kernel(h_ref, wq_ref, keys_ref, s_ref):
    q = jnp.dot(h_ref[...], wq_ref[...], preferred_element_type=F32).astype(BF16)
    for hc in range(N_KEYSETS):
        qs = q[:, hc * HALF_KEY_DIM:(hc + 1) * HALF_KEY_DIM]
        s_ref[hc] = _dot_nt(keys_ref[hc], qs)


def _peer_scores(h16, wq16, keys16):
    t = h16.shape[0]
    tm = SCORE_TM
    assert t % tm == 0 and HALF_KEY_DIM == LANES
    return pl.pallas_call(
        _peer_score_kernel,
        grid=(t // tm,),
        in_specs=[
            pl.BlockSpec((tm, D_MODEL), lambda i: (i, 0)),
            pl.BlockSpec((D_MODEL, PEER_HEADS * PEER_KEY_DIM), lambda i: (0, 0)),
            pl.BlockSpec((N_KEYSETS, PEER_N_KEYS, HALF_KEY_DIM), lambda i: (0, 0, 0)),
        ],
        out_specs=pl.BlockSpec((N_KEYSETS, PEER_N_KEYS, tm), lambda i: (0, 0, i)),
        out_shape=jax.ShapeDtypeStruct((N_KEYSETS, PEER_N_KEYS, t), F32),
        compiler_params=_cparams("parallel"),
        name="peer_scores",
    )(h16, wq16, keys16)


ROUTE_TT = 1024


SUBLANES = 8


def _sort_network(n):
    def merge(lo, hi, r):
        step = r * 2
        if step < hi - lo:
            yield from merge(lo, hi, step)
            yield from merge(lo + r, hi, step)
            yield from ((i, i + r) for i in range(lo + r, hi - r, step))
        else:
            yield (lo, lo + r)

    def sort(lo, hi):
        if hi - lo >= 1:
            mid = lo + (hi - lo) // 2
            yield from sort(lo, mid)
            yield from sort(mid + 1, hi)
            yield from merge(lo, hi, 1)

    return list(sort(0, n - 1))


def _bitonic_merge_network(n):
    pairs, j = [], n // 2
    while j >= 1:
        pairs += [(i, i ^ j) for i in range(n) if i ^ j > i]
        j //= 2
    return pairs


def _compare_exchange_desc(xs, pairs):
    xs = list(xs)
    for lo, hi in pairs:
        xs[lo], xs[hi] = jnp.maximum(xs[lo], xs[hi]), jnp.minimum(xs[lo], xs[hi])
    return xs


def _top_sorted(tiles):
    k = len(tiles)
    top = _compare_exchange_desc(tiles, _sort_network(k))
    shift = 1
    while shift < SUBLANES:
        other = [pltpu.roll(x, shift, 0) for x in top]
        top = [jnp.maximum(top[i], other[k - 1 - i]) for i in range(k)]
        top = _compare_exchange_desc(top, _bitonic_merge_network(k))
        shift *= 2
    return top


def _count_greater(x, top):
    assert len(top) == 16
    b3 = top[7] > x
    b2 = jnp.where(b3, top[11], top[3]) > x
    b1 = jnp.where(b3, jnp.where(b2, top[13], top[9]), jnp.where(b2, top[5], top[1])) > x
    q = [jnp.where(b1, top[4 * i + 2], top[4 * i]) for i in range(4)]
    b0 = jnp.where(b3, jnp.where(b2, q[3], q[2]), jnp.where(b2, q[1], q[0])) > x
    count = (jnp.where(b3, 8.0, 0.0) + jnp.where(b2, 4.0, 0.0)) + (jnp.where(b1, 2.0, 0.0) + jnp.where(b0, 1.0, 0.0))
    return jnp.where(top[15] > x, 16.0, count)


def _top_ranks(v, k):
    n = v.shape[0]
    rows = lax.broadcasted_iota(jnp.int32, v.shape, 0)
    rank = jnp.full(v.shape, k, jnp.int32)
    work = v
    vals = []
    for r in range(k):
        m = jnp.max(work, axis=0, keepdims=True)
        first = jnp.min(jnp.where(work == m, rows, n), axis=0, keepdims=True)
        sel = rows == first
        rank = jnp.where(sel, r, rank)
        work = jnp.where(sel, -jnp.inf, work)
        vals.append(m)
    return jnp.concatenate(vals, axis=0), rank


def _staircase(sv0, sv1, k):
    arow = lax.broadcasted_iota(jnp.int32, sv0.shape, 0)
    count = jnp.zeros(sv0.shape, jnp.int32)
    front = sv0 + sv1[0:1]
    best = front[0:1]
    z = jnp.zeros_like(best)
    for _ in range(k):
        m = jnp.max(front, axis=0, keepdims=True)
        first = jnp.min(jnp.where(front == m, arow, k), axis=0, keepdims=True)
        win = arow == first
        z = z + jnp.exp(m - best)
        count = count + win.astype(jnp.int32)
        nw = jnp.sum(jnp.where(win, count, 0), axis=0, keepdims=True)
        nxt = jnp.sum(jnp.where(arow == nw, sv1, 0.0), axis=0, keepdims=True)
        nxt = jnp.where(nw < k, nxt, -jnp.inf)
        front = jnp.where(win, sv0 + nxt, front)
    return count.astype(F32), z


def _peer_route_kernel(s_ref, n_ref, e0_ref, r1_ref, e1_ref, sv0_s, rank0_s, sv1_s, rank1_s, cnt_s, zinv_s):
    k = PEER_TOPK
    n_keys, tt = s_ref.shape[1], s_ref.shape[2]
    n_tiles = n_keys // SUBLANES
    assert n_tiles == k
    sides = ((sv0_s, rank0_s), (sv1_s, rank1_s))

    def lane_col(c):
        return pl.ds(pl.multiple_of(c * LANES, LANES), LANES)

    def tiles_of(side, col):
        return [s_ref[side, g * SUBLANES:(g + 1) * SUBLANES, col] for g in range(n_tiles)]

    def sort_col(c, tied):
        col = lane_col(c)
        for side, (sv_s, _) in enumerate(sides):
            tiles = tiles_of(side, col)
            top = _top_sorted(tiles)
            for r in range(k):
                sv_s[r:r + 1, col] = top[r][0:1]
            equal_pair = functools.reduce(jnp.logical_or, [top[r] == top[r + 1] for r in range(k - 1)])
            at_least = sum(jnp.where(x >= top[k - 1], 1.0, 0.0) for x in tiles)
            bad = jnp.where(equal_pair, 1.0, 0.0) + jnp.abs(jnp.sum(at_least, axis=0, keepdims=True) - k)
            tied = jnp.maximum(tied, jnp.max(bad))
            if side == 1:
                for g, x in enumerate(tiles):
                    r1_ref[g * SUBLANES:(g + 1) * SUBLANES, col] = _count_greater(x, top).astype(r1_ref.dtype)
        return tied

    tied = lax.fori_loop(0, tt // LANES, sort_col, jnp.zeros((), F32)) > 0

    @pl.when(tied)
    def _():
        for side, (sv_s, rank_s) in enumerate(sides):
            sv, rank = _top_ranks(s_ref[side], k)
            sv_s[...] = sv
            rank_s[...] = rank
        r1_ref[...] = rank1_s[...].astype(F32).astype(r1_ref.dtype)

    count, z = _staircase(sv0_s[...], sv1_s[...], k)
    cnt_s[...] = count
    zinv_s[...] = 1.0 / z

    def emit_col(c, carry):
        col = lane_col(c)
        s0 = s_ref[0, :, col]
        e0_ref[:, col] = jnp.exp(s0 - sv0_s[0:1, col])
        e1_ref[:, col] = (jnp.exp(s_ref[1, :, col] - sv1_s[0:1, col]) * zinv_s[:, col]).astype(e1_ref.dtype)
        n_of_row = jnp.zeros(s0.shape, F32)
        for a in range(k):
            n_of_row = jnp.where(s0 == sv0_s[a:a + 1, col], cnt_s[a:a + 1, col], n_of_row)
        n_ref[:, col] = n_of_row
        return carry

    lax.fori_loop(0, tt // LANES, emit_col, 0)

    @pl.when(tied)
    def _():
        rank0 = rank0_s[...]
        cnt = cnt_s[...]
        n_of_row = jnp.zeros(rank0.shape, F32)
        for a in range(k):
            n_of_row = jnp.where(rank0 == a, cnt[a:a + 1], n_of_row)
        n_ref[...] = n_of_row


def _peer_route(scores):
    t = scores.shape[-1]
    tt = ROUTE_TT
    assert t % tt == 0 and tt % LANES == 0
    out_spec = pl.BlockSpec((None, PEER_N_KEYS, tt), lambda h, i: (h, 0, i))
    shape = (PEER_HEADS, PEER_N_KEYS, t)
    return pl.pallas_call(
        _peer_route_kernel,
        grid=(PEER_HEADS, t // tt),
        in_specs=[pl.BlockSpec((2, PEER_N_KEYS, tt), lambda h, i: (h, 0, i))],
        out_specs=[out_spec] * 4,
        out_shape=[jax.ShapeDtypeStruct(shape, F32), jax.ShapeDtypeStruct(shape, F32),
                   jax.ShapeDtypeStruct(shape, BF16), jax.ShapeDtypeStruct(shape, BF16)],
        scratch_shapes=[pltpu.VMEM((PEER_TOPK, tt), F32), pltpu.VMEM((PEER_N_KEYS, tt), jnp.int32),
                        pltpu.VMEM((PEER_TOPK, tt), F32), pltpu.VMEM((PEER_N_KEYS, tt), jnp.int32),
                        pltpu.VMEM((PEER_TOPK, tt), F32), pltpu.VMEM((1, tt), F32)],
        compiler_params=_cparams("parallel", "parallel"),
        name="peer_route",
    )(scores)


EXPERT_TM = 512
EXPERT_TE = 1024
EXPERT_VMEM_LIMIT = 62 * 1024 * 1024
EXPERT_DRAIN_STEPS = 2
INV_SQRT2 = 1.0 / math.sqrt(2.0)


def _expert_work(s, lag, n_work):
    return jnp.clip(s - lag, 0, n_work - 1)


def _peer_expert_kernel(xt_ref, u_ref, vt_ref, n_ref, e0_ref, r1_ref, e1_ref, h_ref, g_ref, b_ref,
                        o_ref, acc_ref, pre_ref, hid_ref, *, n_blocks, n_work):
    s = pl.program_id(0)
    te = u_ref.shape[0]
    tm = xt_ref.shape[1]
    rows_per_block = te // PEER_N_KEYS
    zero = jnp.zeros((), BF16)
    block_down = _expert_work(s, 2, n_work) % n_blocks
    block_gate = _expert_work(s, 1, n_work) % n_blocks

    @pl.when(s == 0)
    def _():
        pre_ref[...] = jnp.zeros_like(pre_ref)
        hid_ref[...] = jnp.zeros_like(hid_ref)

    @pl.when(block_down == 0)
    def _():
        acc_ref[...] = jnp.zeros_like(acc_ref)

    acc_ref[...] += jnp.dot(vt_ref[...], hid_ref[...], preferred_element_type=F32)

    a = pre_ref[...]
    act = (0.5 * a * (1.0 + lax.erf(a * INV_SQRT2))).astype(BF16)
    first_row = block_gate * rows_per_block
    gates = []
    for ib in range(rows_per_block):
        i = first_row + ib
        w = jnp.zeros((PEER_N_KEYS, tm), BF16)
        for hd in range(PEER_HEADS):
            n_b = jnp.broadcast_to(n_ref[hd, pl.ds(i, 1), :], (PEER_N_KEYS, tm)).astype(BF16)
            e0_b = jnp.broadcast_to(e0_ref[hd, pl.ds(i, 1), :], (PEER_N_KEYS, tm)).astype(BF16)
            w = w + jnp.where(r1_ref[hd] < n_b, e1_ref[hd], zero) * e0_b
        gates.append(w)
    hid_ref[...] = jnp.concatenate(gates, axis=0) * act

    pre_ref[...] = jnp.dot(u_ref[...], xt_ref[...], preferred_element_type=F32)

    @pl.when(jnp.logical_and(block_down == n_blocks - 1, s >= EXPERT_DRAIN_STEPS))
    def _():
        y = DEEPNORM_ALPHA * h_ref[...] + acc_ref[...].T
        o_ref[...] = _layer_norm(y, g_ref[...], b_ref[...])


def _peer_experts(h16t, u16, v16t, n_t, e0_t, r1_t, e1_t, h32, g, b):
    t = h16t.shape[1]
    tm, te = EXPERT_TM, EXPERT_TE
    assert t % tm == 0 and PEER_N_EXPERTS % te == 0 and te % PEER_N_KEYS == 0
    n_blocks = PEER_N_EXPERTS // te
    n_work = (t // tm) * n_blocks

    def item(lag):
        return lambda s: divmod(_expert_work(s, lag, n_work), n_blocks)

    up, gate, down = item(0), item(1), item(2)
    route = pl.BlockSpec((PEER_HEADS, PEER_N_KEYS, tm), lambda s: (0, 0, gate(s)[0]))
    row = pl.BlockSpec((1, D_MODEL), lambda s: (0, 0))
    return pl.pallas_call(
        functools.partial(_peer_expert_kernel, n_blocks=n_blocks, n_work=n_work),
        grid=(n_work + EXPERT_DRAIN_STEPS,),
        in_specs=[
            pl.BlockSpec((D_MODEL, tm), lambda s: (0, up(s)[0])),
            pl.BlockSpec((te, D_MODEL), lambda s: (up(s)[1], 0)),
            pl.BlockSpec((None, D_MODEL, te), lambda s: (down(s)[1], 0, 0)),
            route, route, route, route,
            pl.BlockSpec((tm, D_MODEL), lambda s: (down(s)[0], 0), pipeline_mode=pl.Buffered(1)),
            row, row,
        ],
        out_specs=pl.BlockSpec((tm, D_MODEL), lambda s: (down(s)[0], 0)),
        out_shape=jax.ShapeDtypeStruct((t, D_MODEL), F32),
        scratch_shapes=[pltpu.VMEM((D_MODEL, tm), F32), pltpu.VMEM((te, tm), F32), pltpu.VMEM((te, tm), BF16)],
        compiler_params=_cparams("arbitrary", vmem_limit=EXPERT_VMEM_LIMIT),
        name="peer_experts",
    )(h16t, u16, v16t, n_t, e0_t, r1_t, e1_t, h32, g, b)


def kernel(x, ln_emb_g, ln_emb_b, w_in, dil_norm_g, lambda_q1, lambda_k1, lambda_q2, lambda_k2, subln_g,
           w_out, ln1_g, ln1_b, peer_w_query, peer_sub_keys, peer_u, peer_v, ln2_g, ln2_b):
    bsz, seq, d_model = x.shape
    assert d_model == D_MODEL and w_in.shape[0] == DEPTH == 1
    t = bsz * seq
    row = lambda p: p.reshape(1, -1).astype(F32)
    lam_init = 0.8 - 0.6 * math.exp(-0.3 * 0)

    h0, proj = _inproj(x.reshape(t, D_MODEL), row(ln_emb_g), row(ln_emb_b), w_in[0].astype(BF16), seq)
    proj3 = proj.reshape(bsz, seq, IN_WIDTH)

    mix_a = _dil_attn(proj3, row(dil_norm_g[0])).reshape(t, DIL_WIDTH)
    mix_d = _diff_attn(proj3, row(lambda_q1[0]), row(lambda_k1[0]), row(lambda_q2[0]), row(lambda_k2[0]),
                       row(subln_g[0]), lam_init).reshape(t, DIFF_WIDTH)

    w_out16 = w_out[0].astype(BF16)
    h1, h1_16, h1_16t = _outproj(mix_a, mix_d, w_out16[:DIL_WIDTH], w_out16[DIL_WIDTH:], h0,
                                 row(ln1_g[0]), row(ln1_b[0]))

    keys16 = peer_sub_keys[0].reshape(N_KEYSETS, PEER_N_KEYS, HALF_KEY_DIM).astype(BF16)
    scores = _peer_scores(h1_16, peer_w_query[0].astype(BF16), keys16)
    n_t, e0_t, r1_t, e1_t = _peer_route(scores)
    v16t = peer_v[0].astype(BF16).reshape(PEER_N_EXPERTS // EXPERT_TE, EXPERT_TE, D_MODEL).transpose(0, 2, 1)
    out = _peer_experts(h1_16t, peer_u[0].astype(BF16), v16t, n_t, e0_t, r1_t, e1_t,
                        h1, row(ln2_g[0]), row(ln2_b[0]))
    return out.reshape(bsz, seq, D_MODEL)
```

```python
import functools
import math

import jax
import jax.numpy as jnp
from jax import lax
from jax.experimental import pallas as pl
from jax.experimental.pallas import tpu as pltpu

F32 = jnp.float32
BF16 = jnp.bfloat16

D_MODEL = 2048
DEPTH = 1
HEAD_DIM = 128
DIFF_WIDTH = D_MODEL // 4
DIL_WIDTH = D_MODEL - DIFF_WIDTH
N_DIL_HEADS = DIL_WIDTH // HEAD_DIM
DIL_CONFIGS = ((128, 1), (512, 4), (2048, 16))
DIFF_QK_DIM = 64
DIFF_V_DIM = 2 * DIFF_QK_DIM
N_DIFF_HEADS = DIFF_WIDTH // DIFF_V_DIM
DIFF_QK_WIDTH = N_DIFF_HEADS * 2 * DIFF_QK_DIM
IN_WIDTH = 3 * DIL_WIDTH + 2 * DIFF_QK_WIDTH + DIFF_WIDTH
ROPE_THETA = 500000.0
ROPE_FRACTION = 4
PEER_HEADS = 8
PEER_N_KEYS = 128
PEER_N_EXPERTS = PEER_N_KEYS * PEER_N_KEYS
PEER_KEY_DIM = 256
PEER_TOPK = 16
LN_EPS = 1e-5
NEG_BIG = -1e30
DEEPNORM_ALPHA = (2.0 * DEPTH) ** 0.25

LANES = 128
BF16_ROWS = 16
VMEM_LIMIT = 56 * 1024 * 1024

COL_QA = 0
COL_KA = DIL_WIDTH // LANES
COL_VA = 2 * DIL_WIDTH // LANES
COL_QD = 3 * DIL_WIDTH // LANES
COL_KD = COL_QD + DIFF_QK_WIDTH // LANES
COL_VD = COL_KD + DIFF_QK_WIDTH // LANES


def _cparams(*sem, vmem_limit=VMEM_LIMIT):
    return pltpu.CompilerParams(dimension_semantics=sem, vmem_limit_bytes=vmem_limit)


def _layer_norm(x, g, b):
    mu = jnp.mean(x, -1, keepdims=True)
    xc = x - mu
    var = jnp.mean(xc * xc, -1, keepdims=True)
    return xc * lax.rsqrt(var + LN_EPS) * g + b


def _head_rms(t, g):
    return t * lax.rsqrt(jnp.mean(t * t, -1, keepdims=True) + LN_EPS) * g


def _dot_nt(a, b):
    return lax.dot_general(a, b, (((1,), (1,)), ((), ())), preferred_element_type=F32)


INPROJ_TM = 1024
INPROJ_TN = 512
ROPE_DIL_SHIFT = HEAD_DIM // ROPE_FRACTION // 2
ROPE_DIFF_SHIFT = DIFF_QK_DIM // ROPE_FRACTION // 2


def _rope_tables(seq, period, half):
    rot = 2 * half
    inv_freq = 1.0 / (ROPE_THETA ** (jnp.arange(0, rot, 2, dtype=F32) / rot))
    ang = jnp.arange(seq, dtype=F32)[:, None] * inv_freq[None, :]
    cos, sin = jnp.cos(ang), jnp.sin(ang)
    pad = period - rot
    c = jnp.concatenate([cos, cos, jnp.ones((seq, pad), F32)], axis=1)
    a = jnp.concatenate([-sin, jnp.zeros((seq, half + pad), F32)], axis=1)
    b = jnp.concatenate([jnp.zeros((seq, half), F32), sin, jnp.zeros((seq, pad), F32)], axis=1)
    reps = LANES // period
    return tuple(jnp.tile(t, (1, reps)) for t in (c, a, b))


ROPE_KIND_DIL, ROPE_KIND_DIFF, ROPE_KIND_NONE = 0, 1, 2


def _inproj_kernel(x_ref, g_ref, b_ref, w_ref, c_ref, a_ref, bt_ref, h_ref, o_ref, xs_ref, raw_ref,
                   *, n_col_blocks, kind_of_block):
    s = pl.program_id(0)
    last_tile = pl.num_programs(0) - 2
    j = jnp.minimum(s, last_tile) % n_col_blocks
    j_prev = jnp.maximum(s - 1, 0) % n_col_blocks

    @pl.when(s == 0)
    def _():
        raw_ref[...] = jnp.zeros_like(raw_ref)

    @pl.when(jnp.logical_and(j == 0, s <= last_tile))
    def _():
        h = _layer_norm(x_ref[...], g_ref[...], b_ref[...])
        h_ref[...] = h
        xs_ref[...] = h.astype(BF16)

    shift = jnp.where(kind_of_block(j_prev) == ROPE_KIND_DIFF, ROPE_DIFF_SHIFT, ROPE_DIL_SHIFT)
    c, a, b = c_ref[...], a_ref[...], bt_ref[...]
    for gidx in range(raw_ref.shape[1] // LANES):
        t = raw_ref[:, gidx * LANES:(gidx + 1) * LANES]
        r = t * c + pltpu.roll(t, LANES - shift, 1) * a + pltpu.roll(t, shift, 1) * b
        o_ref[:, gidx * LANES:(gidx + 1) * LANES] = r.astype(o_ref.dtype)

    raw_ref[...] = jnp.dot(xs_ref[...], w_ref[...], preferred_element_type=F32)


def _inproj(x2, g, b, w16, seq):
    t = x2.shape[0]
    tm, tn = INPROJ_TM, INPROJ_TN
    assert t % tm == 0 and seq % tm == 0 and IN_WIDTH % tn == 0
    assert (2 * DIL_WIDTH) % tn == 0 and (3 * DIL_WIDTH) % tn == 0 and DIFF_QK_WIDTH * 2 % tn == 0
    sblocks = seq // tm
    n_col = IN_WIDTH // tn
    n_tiles = (t // tm) * n_col
    n_dil, diff_lo = 2 * DIL_WIDTH // tn, 3 * DIL_WIDTH // tn
    diff_hi = (3 * DIL_WIDTH + 2 * DIFF_QK_WIDTH) // tn

    def kind_of_block(j):
        return jnp.where(j < n_dil, ROPE_KIND_DIL,
                         jnp.where(jnp.logical_and(j >= diff_lo, j < diff_hi), ROPE_KIND_DIFF, ROPE_KIND_NONE))

    def tile(s):
        s = jnp.minimum(s, n_tiles - 1)
        return s // n_col, s % n_col

    def prev_tile(s):
        return tile(jnp.maximum(s - 1, 0))

    ident = (jnp.ones((seq, LANES), F32), jnp.zeros((seq, LANES), F32), jnp.zeros((seq, LANES), F32))
    tabs = [jnp.stack(per_kind) for per_kind in zip(_rope_tables(seq, HEAD_DIM, ROPE_DIL_SHIFT),
                                                     _rope_tables(seq, DIFF_QK_DIM, ROPE_DIFF_SHIFT), ident)]
    tab_spec = pl.BlockSpec(
        (None, tm, LANES), lambda s: (kind_of_block(prev_tile(s)[1]), prev_tile(s)[0] % sblocks, 0))
    return pl.pallas_call(
        functools.partial(_inproj_kernel, n_col_blocks=n_col, kind_of_block=kind_of_block),
        grid=(n_tiles + 1,),
        in_specs=[
            pl.BlockSpec((tm, D_MODEL), lambda s: (tile(s)[0], 0)),
            pl.BlockSpec((1, D_MODEL), lambda s: (0, 0)),
            pl.BlockSpec((1, D_MODEL), lambda s: (0, 0)),
            pl.BlockSpec((D_MODEL, tn), lambda s: (0, tile(s)[1])),
            tab_spec, tab_spec, tab_spec,
        ],
        out_specs=[
            pl.BlockSpec((tm, D_MODEL), lambda s: (tile(s)[0], 0)),
            pl.BlockSpec((tm, tn), lambda s: prev_tile(s)),
        ],
        out_shape=[
            jax.ShapeDtypeStruct((t, D_MODEL), F32),
            jax.ShapeDtypeStruct((t, IN_WIDTH), BF16),
        ],
        scratch_shapes=[pltpu.VMEM((tm, D_MODEL), BF16), pltpu.VMEM((tm, tn), F32)],
        compiler_params=_cparams("arbitrary"),
        name="inproj",
    )(x2, g, b, w16, *tabs)


DIL_BQ = 128
DIL_WIN = 256
DIL_GROUP = 8
DIL_BASE = 4


def _dil_plan(seq, window, d):
    length = seq // d
    half = window // (2 * d)
    bq = min(DIL_BQ, length)
    win = min(DIL_WIN, length)
    nblk = length // bq
    blocks = min(nblk, DIL_GROUP)
    residues = max(1, min(d, DIL_GROUP // nblk))
    assert seq % d == 0 and length % bq == 0 and nblk % blocks == 0 and d % residues == 0
    assert win >= min(length, bq + 2 * half) and half % BF16_ROWS == 0
    return length, half, bq, win, nblk, blocks, residues


def _dil_attn_kernel(q_ref, k_ref, v_ref, g_ref, o_ref,
                     stage, q4, k4, v4, qd, kd, vd, vaug, acc, m_s, l_s, *, seq):
    scale = HEAD_DIM ** -0.5
    quarter = seq // DIL_BASE
    for src, dst in ((q_ref, q4), (k_ref, k4), (v_ref, v4)):
        stage[...] = src[...].astype(F32)
        for r in range(DIL_BASE):
            dst[pl.ds(r * quarter, quarter), :] = stage[pl.ds(r, quarter, stride=DIL_BASE), :]
    vaug[:, :LANES] = v_ref[...]
    vaug[:, LANES:] = jnp.ones((seq, LANES), BF16)
    vd[:, LANES:] = jnp.ones((vd.shape[0], LANES), BF16)

    def run_config(first, window, d):
        length, half, bq, win, nblk, blocks, residues = _dil_plan(seq, window, d)

        def group(rg, bg):
            if d == 1:
                qs, ks, vs = q_ref, k_ref, vaug
            else:
                qs, ks, vs = qd, kd, vd
            items = []
            for j in range(residues):
                r = rg * residues + j
                for u in range(blocks):
                    q0 = pl.multiple_of((bg * blocks + u) * bq, bq)
                    k0 = pl.multiple_of(jnp.clip(q0 - half, 0, length - win), BF16_ROWS)
                    rows = pl.ds(q0, bq) if d == 1 else pl.ds(r + d * q0, bq, stride=d)
                    items.append((j * length, q0, k0, rows))
            scores = []
            for base, q0, k0, _ in items:
                s = _dot_nt(qs[pl.ds(base + q0, bq), :], ks[pl.ds(base + k0, win), :]) * scale
                qpos = q0 + lax.broadcasted_iota(jnp.int32, (bq, win), 0)
                kpos = k0 + lax.broadcasted_iota(jnp.int32, (bq, win), 1)
                scores.append(jnp.where(jnp.abs(qpos - kpos) <= half, s, NEG_BIG))
            m_new = [jnp.broadcast_to(jnp.max(s, -1, keepdims=True), (bq, LANES)) for s in scores]
            if not first:
                m_old = [m_s[rows, :] for _, _, _, rows in items]
                m_new = [jnp.maximum(a, b) for a, b in zip(m_old, m_new)]
            def widen(m):
                return jnp.tile(m, (1, win // LANES)) if win % LANES == 0 else m[:, :1]
            probs = [jnp.exp(s - widen(m)).astype(BF16) for s, m in zip(scores, m_new)]
            pv = [jnp.dot(p, vs[pl.ds(base + k0, win), :], preferred_element_type=F32)
                  for p, (base, _, k0, _) in zip(probs, items)]
            for idx, (_, _, _, rows) in enumerate(items):
                a_new, l_new = pv[idx][:, :LANES], pv[idx][:, LANES:]
                if not first:
                    alpha = jnp.exp(m_old[idx] - m_new[idx])
                    a_new = alpha * acc[rows, :] + a_new
                    l_new = alpha * l_s[rows, :] + l_new
                acc[rows, :] = a_new
                m_s[rows, :] = m_new[idx]
                l_s[rows, :] = l_new

        def residue_group(rg, carry):
            if d > 1:
                for j in range(residues):
                    r = rg * residues + j
                    sub = pl.ds((r % DIL_BASE) * quarter + r // DIL_BASE, length, stride=d // DIL_BASE)
                    dst = pl.ds(j * length, length)
                    qd[dst, :] = q4[sub, :].astype(BF16)
                    kd[dst, :] = k4[sub, :].astype(BF16)
                    vd[dst, :LANES] = v4[sub, :].astype(BF16)
            if nblk == blocks:
                group(rg, 0)
            else:
                lax.fori_loop(0, nblk // blocks, lambda bg, c: (group(rg, bg), c)[1], 0)
            return carry

        if d == residues:
            residue_group(0, 0)
        else:
            lax.fori_loop(0, d // residues, residue_group, 0)

    for ci, (window, d) in enumerate(sorted(DIL_CONFIGS, key=lambda wd: -wd[1])):
        run_config(ci == 0, window, d)

    o_ref[...] = _head_rms(acc[...] / l_s[...], g_ref[...]).astype(o_ref.dtype)


def _dil_attn(proj3, g):
    bsz, seq, _ = proj3.shape
    assert all(d == 1 or d % DIL_BASE == 0 for _, d in DIL_CONFIGS) and seq % (DIL_BASE * SUBLANES) == 0
    plans = [_dil_plan(seq, window, d) for window, d in DIL_CONFIGS if d > 1]
    sub_rows = max(length * residues for length, _, _, _, _, _, residues in plans)

    def spec(col0):
        return pl.BlockSpec((None, seq, LANES), lambda b, h: (b, 0, col0 + h))

    return pl.pallas_call(
        functools.partial(_dil_attn_kernel, seq=seq),
        grid=(bsz, N_DIL_HEADS),
        in_specs=[spec(COL_QA), spec(COL_KA), spec(COL_VA), pl.BlockSpec((1, LANES), lambda b, h: (0, 0))],
        out_specs=pl.BlockSpec((None, seq, LANES), lambda b, h: (b, 0, h)),
        out_shape=jax.ShapeDtypeStruct((bsz, seq, DIL_WIDTH), BF16),
        scratch_shapes=[pltpu.VMEM((seq, LANES), F32)] * 4
        + [pltpu.VMEM((sub_rows, LANES), BF16)] * 2
        + [pltpu.VMEM((sub_rows, 2 * LANES), BF16), pltpu.VMEM((seq, 2 * LANES), BF16)]
        + [pltpu.VMEM((seq, LANES), F32)] * 3,
        compiler_params=_cparams("parallel", "parallel"),
        name="dil_attn",
    )(proj3, proj3, proj3, g)


DIFF_TQ = 512
DIFF_KEY_CHUNK = 1024


def _diff_attn_kernel(q_ref, k_ref, v_ref, lq1, lk1, lq2, lk2, g_ref, o_ref, vaug, s_scr, mx_scr, *, lam_init):
    i = pl.program_id(2)
    last = pl.num_programs(2) - 1
    seq = k_ref.shape[0]
    kc = min(DIFF_KEY_CHUNK, seq)
    assert seq % kc == 0
    chunks = [pl.ds(c * kc, kc) for c in range(seq // kc)]
    scale = DIFF_QK_DIM ** -0.5
    assert 2.0 ** round(math.log2(scale)) == scale

    def step(finish, score):
        if score:
            q = q_ref[...] * scale
            lo = lax.broadcasted_iota(jnp.int32, q.shape, 1) < DIFF_QK_DIM
            zero = jnp.zeros_like(q)
            q_maps = (jnp.where(lo, q, zero), jnp.where(lo, zero, q))
        outs = []
        for mp in range(2):
            pv, m_next = None, None
            if finish:
                m_prev = mx_scr[mp][:, :1]
            for c in chunks:
                if finish:
                    p = jnp.exp(s_scr[mp, :, c] - m_prev).astype(BF16)
                    part = jnp.dot(p, vaug[c, :], preferred_element_type=F32)
                    pv = part if pv is None else pv + part
                if score:
                    s_new = _dot_nt(q_maps[mp], k_ref[c, :])
                    s_scr[mp, :, c] = s_new
                    m_chunk = jnp.max(s_new, -1, keepdims=True)
                    m_next = m_chunk if m_next is None else jnp.maximum(m_next, m_chunk)
            if score:
                mx_scr[mp] = jnp.broadcast_to(m_next, mx_scr.shape[1:])
            if finish:
                outs.append(pv[:, :LANES] / pv[:, LANES:])
        if finish:
            o0, o1 = outs
            lam = (jnp.exp(jnp.sum(lq1[...] * lk1[...], keepdims=True))
                   - jnp.exp(jnp.sum(lq2[...] * lk2[...], keepdims=True)) + lam_init)
            a = o0 - lam * o1
            o_ref[...] = (_head_rms(a, g_ref[...]) * (1.0 - lam_init)).astype(o_ref.dtype)

    @pl.when(i == 0)
    def _():
        vaug[:, :LANES] = v_ref[...]
        vaug[:, LANES:] = jnp.ones((seq, LANES), BF16)
        step(finish=False, score=True)

    @pl.when(jnp.logical_and(i > 0, i < last))
    def _():
        step(finish=True, score=True)

    @pl.when(i == last)
    def _():
        step(finish=True, score=False)


def _diff_attn(proj3, lq1, lk1, lq2, lk2, g, lam_init):
    bsz, seq, _ = proj3.shape
    tq = DIFF_TQ
    assert seq % tq == 0
    n_tiles = seq // tq
    vec = pl.BlockSpec((1, DIFF_QK_DIM), lambda b, h, i: (0, 0))
    return pl.pallas_call(
        functools.partial(_diff_attn_kernel, lam_init=lam_init),
        grid=(bsz, N_DIFF_HEADS, n_tiles + 1),
        in_specs=[
            pl.BlockSpec((None, tq, LANES), lambda b, h, i: (b, jnp.minimum(i, n_tiles - 1), COL_QD + h)),
            pl.BlockSpec((None, seq, LANES), lambda b, h, i: (b, 0, COL_KD + h)),
            pl.BlockSpec((None, seq, LANES), lambda b, h, i: (b, 0, COL_VD + h)),
            vec, vec, vec, vec,
            pl.BlockSpec((1, DIFF_V_DIM), lambda b, h, i: (0, 0)),
        ],
        out_specs=pl.BlockSpec((None, tq, LANES), lambda b, h, i: (b, jnp.maximum(i - 1, 0), h)),
        out_shape=jax.ShapeDtypeStruct((bsz, seq, DIFF_WIDTH), BF16),
        scratch_shapes=[pltpu.VMEM((seq, 2 * LANES), BF16), pltpu.VMEM((2, tq, seq), F32),
                        pltpu.VMEM((2, tq, LANES), F32)],
        compiler_params=_cparams("parallel", "parallel", "arbitrary"),
        name="diff_attn",
    )(proj3, proj3, proj3, lq1, lk1, lq2, lk2, g)


OUTPROJ_TM = 256


def _outproj_kernel(ma_ref, md_ref, wa_ref, wd_ref, h_ref, g_ref, b_ref, o32_ref, o16_ref, o16t_ref, raw_ref):
    @pl.when(pl.program_id(0) == 0)
    def _():
        raw_ref[...] = jnp.zeros_like(raw_ref)

    h = _layer_norm(DEEPNORM_ALPHA * h_ref[...] + raw_ref[...], g_ref[...], b_ref[...])
    o32_ref[...] = h
    o16_ref[...] = h.astype(BF16)
    o16t_ref[...] = h.T.astype(BF16)

    mix = jnp.dot(ma_ref[...], wa_ref[...], preferred_element_type=F32)
    raw_ref[...] = mix + jnp.dot(md_ref[...], wd_ref[...], preferred_element_type=F32)


def _outproj(mix_a, mix_d, w_a, w_d, h0, g, b):
    t = h0.shape[0]
    tm = OUTPROJ_TM
    assert t % tm == 0
    n_tiles = t // tm
    cur = lambda s: jnp.minimum(s, n_tiles - 1)
    prev = lambda s: jnp.maximum(s - 1, 0)
    row = pl.BlockSpec((1, D_MODEL), lambda s: (0, 0))
    return pl.pallas_call(
        _outproj_kernel,
        grid=(n_tiles + 1,),
        in_specs=[
            pl.BlockSpec((tm, DIL_WIDTH), lambda s: (cur(s), 0)),
            pl.BlockSpec((tm, DIFF_WIDTH), lambda s: (cur(s), 0)),
            pl.BlockSpec((DIL_WIDTH, D_MODEL), lambda s: (0, 0)),
            pl.BlockSpec((DIFF_WIDTH, D_MODEL), lambda s: (0, 0)),
            pl.BlockSpec((tm, D_MODEL), lambda s: (prev(s), 0)),
            row, row,
        ],
        out_specs=[pl.BlockSpec((tm, D_MODEL), lambda s: (prev(s), 0))] * 2
        + [pl.BlockSpec((D_MODEL, tm), lambda s: (0, prev(s)))],
        out_shape=[jax.ShapeDtypeStruct((t, D_MODEL), F32), jax.ShapeDtypeStruct((t, D_MODEL), BF16),
                   jax.ShapeDtypeStruct((D_MODEL, t), BF16)],
        scratch_shapes=[pltpu.VMEM((tm, D_MODEL), F32)],
        compiler_params=_cparams("arbitrary"),
        name="outproj",
    )(mix_a, mix_d, w_a, w_d, h0, g, b)


SCORE_TM = 512
N_KEYSETS = 2 * PEER_HEADS
HALF_KEY_DIM = PEER_KEY_DIM // 2


def _peer_score_kernel(h_ref, wq_ref, keys_ref, s_ref):
    q = jnp.dot(h_ref[...], wq_ref[...], preferred_element_type=F32).astype(BF16)
    for hc in range(N_KEYSETS):
        qs = q[:, hc * HALF_KEY_DIM:(hc + 1) * HALF_KEY_DIM]
        s_ref[hc] = _dot_nt(keys_ref[hc], qs)


def _peer_scores(h16, wq16, keys16):
    t = h16.shape[0]
    tm = SCORE_TM
    assert t % tm == 0 and HALF_KEY_DIM == LANES
    return pl.pallas_call(
        _peer_score_kernel,
        grid=(t // tm,),
        in_specs=[
            pl.BlockSpec((tm, D_MODEL), lambda i: (i, 0)),
            pl.BlockSpec((D_MODEL, PEER_HEADS * PEER_KEY_DIM), lambda i: (0, 0)),
            pl.BlockSpec((N_KEYSETS, PEER_N_KEYS, HALF_KEY_DIM), lambda i: (0, 0, 0)),
        ],
        out_specs=pl.BlockSpec((N_KEYSETS, PEER_N_KEYS, tm), lambda i: (0, 0, i)),
        out_shape=jax.ShapeDtypeStruct((N_KEYSETS, PEER_N_KEYS, t), F32),
        compiler_params=_cparams("parallel"),
        name="peer_scores",
    )(h16, wq16, keys16)


ROUTE_TT = 1024


SUBLANES = 8


def _sort_network(n):
    def merge(lo, hi, r):
        step = r * 2
        if step < hi - lo:
            yield from merge(lo, hi, step)
            yield from merge(lo + r, hi, step)
            yield from ((i, i + r) for i in range(lo + r, hi - r, step))
        else:
            yield (lo, lo + r)

    def sort(lo, hi):
        if hi - lo >= 1:
            mid = lo + (hi - lo) // 2
            yield from sort(lo, mid)
            yield from sort(mid + 1, hi)
            yield from merge(lo, hi, 1)

    return list(sort(0, n - 1))


def _bitonic_merge_network(n):
    pairs, j = [], n // 2
    while j >= 1:
        pairs += [(i, i ^ j) for i in range(n) if i ^ j > i]
        j //= 2
    return pairs


def _compare_exchange_desc(xs, pairs):
    xs = list(xs)
    for lo, hi in pairs:
        xs[lo], xs[hi] = jnp.maximum(xs[lo], xs[hi]), jnp.minimum(xs[lo], xs[hi])
    return xs


def _top_sorted(tiles):
    k = len(tiles)
    top = _compare_exchange_desc(tiles, _sort_network(k))
    shift = 1
    while shift < SUBLANES:
        other = [pltpu.roll(x, shift, 0) for x in top]
        top = [jnp.maximum(top[i], other[k - 1 - i]) for i in range(k)]
        top = _compare_exchange_desc(top, _bitonic_merge_network(k))
        shift *= 2
    return top


def _count_greater(x, top):
    assert len(top) == 16
    b3 = top[7] > x
    b2 = jnp.where(b3, top[11], top[3]) > x
    b1 = jnp.where(b3, jnp.where(b2, top[13], top[9]), jnp.where(b2, top[5], top[1])) > x
    q = [jnp.where(b1, top[4 * i + 2], top[4 * i]) for i in range(4)]
    b0 = jnp.where(b3, jnp.where(b2, q[3], q[2]), jnp.where(b2, q[1], q[0])) > x
    count = (jnp.where(b3, 8.0, 0.0) + jnp.where(b2, 4.0, 0.0)) + (jnp.where(b1, 2.0, 0.0) + jnp.where(b0, 1.0, 0.0))
    return jnp.where(top[15] > x, 16.0, count)


def _top_ranks(v, k):
    n = v.shape[0]
    rows = lax.broadcasted_iota(jnp.int32, v.shape, 0)
    rank = jnp.full(v.shape, k, jnp.int32)
    work = v
    vals = []
    for r in range(k):
        m = jnp.max(work, axis=0, keepdims=True)
        first = jnp.min(jnp.where(work == m, rows, n), axis=0, keepdims=True)
        sel = rows == first
        rank = jnp.where(sel, r, rank)
        work = jnp.where(sel, -jnp.inf, work)
        vals.append(m)
    return jnp.concatenate(vals, axis=0), rank


def _staircase(sv0, sv1, k):
    arow = lax.broadcasted_iota(jnp.int32, sv0.shape, 0)
    count = jnp.zeros(sv0.shape, jnp.int32)
    front = sv0 + sv1[0:1]
    best = front[0:1]
    z = jnp.zeros_like(best)
    for _ in range(k):
        m = jnp.max(front, axis=0, keepdims=True)
        first = jnp.min(jnp.where(front == m, arow, k), axis=0, keepdims=True)
        win = arow == first
        z = z + jnp.exp(m - best)
        count = count + win.astype(jnp.int32)
        nw = jnp.sum(jnp.where(win, count, 0), axis=0, keepdims=True)
        nxt = jnp.sum(jnp.where(arow == nw, sv1, 0.0), axis=0, keepdims=True)
        nxt = jnp.where(nw < k, nxt, -jnp.inf)
        front = jnp.where(win, sv0 + nxt, front)
    return count.astype(F32), z


def _peer_route_kernel(s_ref, n_ref, e0_ref, r1_ref, e1_ref, sv0_s, rank0_s, sv1_s, rank1_s, cnt_s, zinv_s):
    k = PEER_TOPK
    n_keys, tt = s_ref.shape[1], s_ref.shape[2]
    n_tiles = n_keys // SUBLANES
    assert n_tiles == k
    sides = ((sv0_s, rank0_s), (sv1_s, rank1_s))

    def lane_col(c):
        return pl.ds(pl.multiple_of(c * LANES, LANES), LANES)

    def tiles_of(side, col):
        return [s_ref[side, g * SUBLANES:(g + 1) * SUBLANES, col] for g in range(n_tiles)]

    def sort_col(c, tied):
        col = lane_col(c)
        for side, (sv_s, _) in enumerate(sides):
            tiles = tiles_of(side, col)
            top = _top_sorted(tiles)
            for r in range(k):
                sv_s[r:r + 1, col] = top[r][0:1]
            equal_pair = functools.reduce(jnp.logical_or, [top[r] == top[r + 1] for r in range(k - 1)])
            at_least = sum(jnp.where(x >= top[k - 1], 1.0, 0.0) for x in tiles)
            bad = jnp.where(equal_pair, 1.0, 0.0) + jnp.abs(jnp.sum(at_least, axis=0, keepdims=True) - k)
            tied = jnp.maximum(tied, jnp.max(bad))
            if side == 1:
                for g, x in enumerate(tiles):
                    r1_ref[g * SUBLANES:(g + 1) * SUBLANES, col] = _count_greater(x, top).astype(r1_ref.dtype)
        return tied

    tied = lax.fori_loop(0, tt // LANES, sort_col, jnp.zeros((), F32)) > 0

    @pl.when(tied)
    def _():
        for side, (sv_s, rank_s) in enumerate(sides):
            sv, rank = _top_ranks(s_ref[side], k)
            sv_s[...] = sv
            rank_s[...] = rank
        r1_ref[...] = rank1_s[...].astype(F32).astype(r1_ref.dtype)

    count, z = _staircase(sv0_s[...], sv1_s[...], k)
    cnt_s[...] = count
    zinv_s[...] = 1.0 / z

    def emit_col(c, carry):
        col = lane_col(c)
        s0 = s_ref[0, :, col]
        e0_ref[:, col] = jnp.exp(s0 - sv0_s[0:1, col])
        e1_ref[:, col] = (jnp.exp(s_ref[1, :, col] - sv1_s[0:1, col]) * zinv_s[:, col]).astype(e1_ref.dtype)
        n_of_row = jnp.zeros(s0.shape, F32)
        for a in range(k):
            n_of_row = jnp.where(s0 == sv0_s[a:a + 1, col], cnt_s[a:a + 1, col], n_of_row)
        n_ref[:, col] = n_of_row
        return carry

    lax.fori_loop(0, tt // LANES, emit_col, 0)

    @pl.when(tied)
    def _():
        rank0 = rank0_s[...]
        cnt = cnt_s[...]
        n_of_row = jnp.zeros(rank0.shape, F32)
        for a in range(k):
            n_of_row = jnp.where(rank0 == a, cnt[a:a + 1], n_of_row)
        n_ref[...] = n_of_row


def _peer_route(scores):
    t = scores.shape[-1]
    tt = ROUTE_TT
    assert t % tt == 0 and tt % LANES == 0
    out_spec = pl.BlockSpec((None, PEER_N_KEYS, tt), lambda h, i: (h, 0, i))
    shape = (PEER_HEADS, PEER_N_KEYS, t)
    return pl.pallas_call(
        _peer_route_kernel,
        grid=(PEER_HEADS, t // tt),
        in_specs=[pl.BlockSpec((2, PEER_N_KEYS, tt), lambda h, i: (h, 0, i))],
        out_specs=[out_spec] * 4,
        out_shape=[jax.ShapeDtypeStruct(shape, F32), jax.ShapeDtypeStruct(shape, F32),
                   jax.ShapeDtypeStruct(shape, BF16), jax.ShapeDtypeStruct(shape, BF16)],
        scratch_shapes=[pltpu.VMEM((PEER_TOPK, tt), F32), pltpu.VMEM((PEER_N_KEYS, tt), jnp.int32),
                        pltpu.VMEM((PEER_TOPK, tt), F32), pltpu.VMEM((PEER_N_KEYS, tt), jnp.int32),
                        pltpu.VMEM((PEER_TOPK, tt), F32), pltpu.VMEM((1, tt), F32)],
        compiler_params=_cparams("parallel", "parallel"),
        name="peer_route",
    )(scores)


EXPERT_TM = 512
EXPERT_TE = 1024
EXPERT_VMEM_LIMIT = 62 * 1024 * 1024
EXPERT_DRAIN_STEPS = 2
INV_SQRT2 = 1.0 / math.sqrt(2.0)


def _expert_work(s, lag, n_work):
    return jnp.clip(s - lag, 0, n_work - 1)


def _peer_expert_kernel(xt_ref, u_ref, vt_ref, n_ref, e0_ref, r1_ref, e1_ref, h_ref, g_ref, b_ref,
                        o_ref, acc_ref, pre_ref, hid_ref, *, n_blocks, n_work):
    s = pl.program_id(0)
    te = u_ref.shape[0]
    tm = xt_ref.shape[1]
    rows_per_block = te // PEER_N_KEYS
    zero = jnp.zeros((), BF16)
    block_down = _expert_work(s, 2, n_work) % n_blocks
    block_gate = _expert_work(s, 1, n_work) % n_blocks

    @pl.when(s == 0)
    def _():
        pre_ref[...] = jnp.zeros_like(pre_ref)
        hid_ref[...] = jnp.zeros_like(hid_ref)

    @pl.when(block_down == 0)
    def _():
        acc_ref[...] = jnp.zeros_like(acc_ref)

    acc_ref[...] += jnp.dot(vt_ref[...], hid_ref[...], preferred_element_type=F32)

    a = pre_ref[...]
    act = (0.5 * a * (1.0 + lax.erf(a * INV_SQRT2))).astype(BF16)
    first_row = block_gate * rows_per_block
    gates = []
    for ib in range(rows_per_block):
        i = first_row + ib
        w = jnp.zeros((PEER_N_KEYS, tm), BF16)
        for hd in range(PEER_HEADS):
            n_b = jnp.broadcast_to(n_ref[hd, pl.ds(i, 1), :], (PEER_N_KEYS, tm)).astype(BF16)
            e0_b = jnp.broadcast_to(e0_ref[hd, pl.ds(i, 1), :], (PEER_N_KEYS, tm)).astype(BF16)
            w = w + jnp.where(r1_ref[hd] < n_b, e1_ref[hd], zero) * e0_b
        gates.append(w)
    hid_ref[...] = jnp.concatenate(gates, axis=0) * act

    pre_ref[...] = jnp.dot(u_ref[...], xt_ref[...], preferred_element_type=F32)

    @pl.when(jnp.logical_and(block_down == n_blocks - 1, s >= EXPERT_DRAIN_STEPS))
    def _():
        y = DEEPNORM_ALPHA * h_ref[...] + acc_ref[...].T
        o_ref[...] = _layer_norm(y, g_ref[...], b_ref[...])


def _peer_experts(h16t, u16, v16t, n_t, e0_t, r1_t, e1_t, h32, g, b):
    t = h16t.shape[1]
    tm, te = EXPERT_TM, EXPERT_TE
    assert t % tm == 0 and PEER_N_EXPERTS % te == 0 and te % PEER_N_KEYS == 0
    n_blocks = PEER_N_EXPERTS // te
    n_work = (t // tm) * n_blocks

    def item(lag):
        return lambda s: divmod(_expert_work(s, lag, n_work), n_blocks)

    up, gate, down = item(0), item(1), item(2)
    route = pl.BlockSpec((PEER_HEADS, PEER_N_KEYS, tm), lambda s: (0, 0, gate(s)[0]))
    row = pl.BlockSpec((1, D_MODEL), lambda s: (0, 0))
    return pl.pallas_call(
        functools.partial(_peer_expert_kernel, n_blocks=n_blocks, n_work=n_work),
        grid=(n_work + EXPERT_DRAIN_STEPS,),
        in_specs=[
            pl.BlockSpec((D_MODEL, tm), lambda s: (0, up(s)[0])),
            pl.BlockSpec((te, D_MODEL), lambda s: (up(s)[1], 0)),
            pl.BlockSpec((None, D_MODEL, te), lambda s: (down(s)[1], 0, 0)),
            route, route, route, route,
            pl.BlockSpec((tm, D_MODEL), lambda s: (down(s)[0], 0), pipeline_mode=pl.Buffered(1)),
            row, row,
        ],
        out_specs=pl.BlockSpec((tm, D_MODEL), lambda s: (down(s)[0], 0)),
        out_shape=jax.ShapeDtypeStruct((t, D_MODEL), F32),
        scratch_shapes=[pltpu.VMEM((D_MODEL, tm), F32), pltpu.VMEM((te, tm), F32), pltpu.VMEM((te, tm), BF16)],
        compiler_params=_cparams("arbitrary", vmem_limit=EXPERT_VMEM_LIMIT),
        name="peer_experts",
    )(h16t, u16, v16t, n_t, e0_t, r1_t, e1_t, h32, g, b)


def kernel(x, ln_emb_g, ln_emb_b, w_in, dil_norm_g, lambda_q1, lambda_k1, lambda_q2, lambda_k2, subln_g,
           w_out, ln1_g, ln1_b, peer_w_query, peer_sub_keys, peer_u, peer_v, ln2_g, ln2_b):
    bsz, seq, d_model = x.shape
    assert d_model == D_MODEL and w_in.shape[0] == DEPTH == 1
    t = bsz * seq
    row = lambda p: p.reshape(1, -1).astype(F32)
    lam_init = 0.8 - 0.6 * math.exp(-0.3 * 0)

    h0, proj = _inproj(x.reshape(t, D_MODEL), row(ln_emb_g), row(ln_emb_b), w_in[0].astype(BF16), seq)
    proj3 = proj.reshape(bsz, seq, IN_WIDTH)

    mix_a = _dil_attn(proj3, row(dil_norm_g[0])).reshape(t, DIL_WIDTH)
    mix_d = _diff_attn(proj3, row(lambda_q1[0]), row(lambda_k1[0]), row(lambda_q2[0]), row(lambda_k2[0]),
                       row(subln_g[0]), lam_init).reshape(t, DIFF_WIDTH)

    w_out16 = w_out[0].astype(BF16)
    h1, h1_16, h1_16t = _outproj(mix_a, mix_d, w_out16[:DIL_WIDTH], w_out16[DIL_WIDTH:], h0,
                                 row(ln1_g[0]), row(ln1_b[0]))

    keys16 = peer_sub_keys[0].reshape(N_KEYSETS, PEER_N_KEYS, HALF_KEY_DIM).astype(BF16)
    scores = _peer_scores(h1_16, peer_w_query[0].astype(BF16), keys16)
    n_t, e0_t, r1_t, e1_t = _peer_route(scores)
    v16t = peer_v[0].astype(BF16).reshape(PEER_N_EXPERTS // EXPERT_TE, EXPERT_TE, D_MODEL).transpose(0, 2, 1)
    out = _peer_experts(h1_16t, peer_u[0].astype(BF16), v16t, n_t, e0_t, r1_t, e1_t,
                        h1, row(ln2_g[0]), row(ln2_b[0]))
    return out.reshape(bsz, seq, D_MODEL)
```

```python
import functools
import math

import jax
import jax.numpy as jnp
from jax import lax
from jax.experimental import pallas as pl
from jax.experimental.pallas import tpu as pltpu

F32 = jnp.float32
BF16 = jnp.bfloat16

D_MODEL = 2048
DEPTH = 1
HEAD_DIM = 128
DIFF_WIDTH = D_MODEL // 4
DIL_WIDTH = D_MODEL - DIFF_WIDTH
N_DIL_HEADS = DIL_WIDTH // HEAD_DIM
DIL_CONFIGS = ((128, 1), (512, 4), (2048, 16))
DIFF_QK_DIM = 64
DIFF_V_DIM = 2 * DIFF_QK_DIM
N_DIFF_HEADS = DIFF_WIDTH // DIFF_V_DIM
DIFF_QK_WIDTH = N_DIFF_HEADS * 2 * DIFF_QK_DIM
IN_WIDTH = 3 * DIL_WIDTH + 2 * DIFF_QK_WIDTH + DIFF_WIDTH
ROPE_THETA = 500000.0
ROPE_FRACTION = 4
PEER_HEADS = 8
PEER_N_KEYS = 128
PEER_N_EXPERTS = PEER_N_KEYS * PEER_N_KEYS
PEER_KEY_DIM = 256
PEER_TOPK = 16
LN_EPS = 1e-5
NEG_BIG = -1e30
DEEPNORM_ALPHA = (2.0 * DEPTH) ** 0.25

LANES = 128
BF16_ROWS = 16
VMEM_LIMIT = 56 * 1024 * 1024

COL_QA = 0
COL_KA = DIL_WIDTH // LANES
COL_VA = 2 * DIL_WIDTH // LANES
COL_QD = 3 * DIL_WIDTH // LANES
COL_KD = COL_QD + DIFF_QK_WIDTH // LANES
COL_VD = COL_KD + DIFF_QK_WIDTH // LANES


def _cparams(*sem, vmem_limit=VMEM_LIMIT):
    return pltpu.CompilerParams(dimension_semantics=sem, vmem_limit_bytes=vmem_limit)


def _layer_norm(x, g, b):
    mu = jnp.mean(x, -1, keepdims=True)
    xc = x - mu
    var = jnp.mean(xc * xc, -1, keepdims=True)
    return xc * lax.rsqrt(var + LN_EPS) * g + b


def _head_rms(t, g):
    return t * lax.rsqrt(jnp.mean(t * t, -1, keepdims=True) + LN_EPS) * g


def _dot_nt(a, b):
    return lax.dot_general(a, b, (((1,), (1,)), ((), ())), preferred_element_type=F32)


INPROJ_TM = 1024
INPROJ_TN = 512
ROPE_DIL_SHIFT = HEAD_DIM // ROPE_FRACTION // 2
ROPE_DIFF_SHIFT = DIFF_QK_DIM // ROPE_FRACTION // 2


def _rope_tables(seq, period, half):
    rot = 2 * half
    inv_freq = 1.0 / (ROPE_THETA ** (jnp.arange(0, rot, 2, dtype=F32) / rot))
    ang = jnp.arange(seq, dtype=F32)[:, None] * inv_freq[None, :]
    cos, sin = jnp.cos(ang), jnp.sin(ang)
    pad = period - rot
    c = jnp.concatenate([cos, cos, jnp.ones((seq, pad), F32)], axis=1)
    a = jnp.concatenate([-sin, jnp.zeros((seq, half + pad), F32)], axis=1)
    b = jnp.concatenate([jnp.zeros((seq, half), F32), sin, jnp.zeros((seq, pad), F32)], axis=1)
    reps = LANES // period
    return tuple(jnp.tile(t, (1, reps)) for t in (c, a, b))


ROPE_KIND_DIL, ROPE_KIND_DIFF, ROPE_KIND_NONE = 0, 1, 2


def _inproj_kernel(x_ref, g_ref, b_ref, w_ref, c_ref, a_ref, bt_ref, h_ref, o_ref, xs_ref, raw_ref,
                   *, n_col_blocks, kind_of_block):
    s = pl.program_id(0)
    last_tile = pl.num_programs(0) - 2
    j = jnp.minimum(s, last_tile) % n_col_blocks
    j_prev = jnp.maximum(s - 1, 0) % n_col_blocks

    @pl.when(s == 0)
    def _():
        raw_ref[...] = jnp.zeros_like(raw_ref)

    @pl.when(jnp.logical_and(j == 0, s <= last_tile))
    def _():
        h = _layer_norm(x_ref[...], g_ref[...], b_ref[...])
        h_ref[...] = h
        xs_ref[...] = h.astype(BF16)

    shift = jnp.where(kind_of_block(j_prev) == ROPE_KIND_DIFF, ROPE_DIFF_SHIFT, ROPE_DIL_SHIFT)
    c, a, b = c_ref[...], a_ref[...], bt_ref[...]
    for gidx in range(raw_ref.shape[1] // LANES):
        t = raw_ref[:, gidx * LANES:(gidx + 1) * LANES]
        r = t * c + pltpu.roll(t, LANES - shift, 1) * a + pltpu.roll(t, shift, 1) * b
        o_ref[:, gidx * LANES:(gidx + 1) * LANES] = r.astype(o_ref.dtype)

    raw_ref[...] = jnp.dot(xs_ref[...], w_ref[...], preferred_element_type=F32)


def _inproj(x2, g, b, w16, seq):
    t = x2.shape[0]
    tm, tn = INPROJ_TM, INPROJ_TN
    assert t % tm == 0 and seq % tm == 0 and IN_WIDTH % tn == 0
    assert (2 * DIL_WIDTH) % tn == 0 and (3 * DIL_WIDTH) % tn == 0 and DIFF_QK_WIDTH * 2 % tn == 0
    sblocks = seq // tm
    n_col = IN_WIDTH // tn
    n_tiles = (t // tm) * n_col
    n_dil, diff_lo = 2 * DIL_WIDTH // tn, 3 * DIL_WIDTH // tn
    diff_hi = (3 * DIL_WIDTH + 2 * DIFF_QK_WIDTH) // tn

    def kind_of_block(j):
        return jnp.where(j < n_dil, ROPE_KIND_DIL,
                         jnp.where(jnp.logical_and(j >= diff_lo, j < diff_hi), ROPE_KIND_DIFF, ROPE_KIND_NONE))

    def tile(s):
        s = jnp.minimum(s, n_tiles - 1)
        return s // n_col, s % n_col

    def prev_tile(s):
        return tile(jnp.maximum(s - 1, 0))

    ident = (jnp.ones((seq, LANES), F32), jnp.zeros((seq, LANES), F32), jnp.zeros((seq, LANES), F32))
    tabs = [jnp.stack(per_kind) for per_kind in zip(_rope_tables(seq, HEAD_DIM, ROPE_DIL_SHIFT),
                                                     _rope_tables(seq, DIFF_QK_DIM, ROPE_DIFF_SHIFT), ident)]
    tab_spec = pl.BlockSpec(
        (None, tm, LANES), lambda s: (kind_of_block(prev_tile(s)[1]), prev_tile(s)[0] % sblocks, 0))
    return pl.pallas_call(
        functools.partial(_inproj_kernel, n_col_blocks=n_col, kind_of_block=kind_of_block),
        grid=(n_tiles + 1,),
        in_specs=[
            pl.BlockSpec((tm, D_MODEL), lambda s: (tile(s)[0], 0)),
            pl.BlockSpec((1, D_MODEL), lambda s: (0, 0)),
            pl.BlockSpec((1, D_MODEL), lambda s: (0, 0)),
            pl.BlockSpec((D_MODEL, tn), lambda s: (0, tile(s)[1])),
            tab_spec, tab_spec, tab_spec,
        ],
        out_specs=[
            pl.BlockSpec((tm, D_MODEL), lambda s: (tile(s)[0], 0)),
            pl.BlockSpec((tm, tn), lambda s: prev_tile(s)),
        ],
        out_shape=[
            jax.ShapeDtypeStruct((t, D_MODEL), F32),
            jax.ShapeDtypeStruct((t, IN_WIDTH), BF16),
        ],
        scratch_shapes=[pltpu.VMEM((tm, D_MODEL), BF16), pltpu.VMEM((tm, tn), F32)],
        compiler_params=_cparams("arbitrary"),
        name="inproj",
    )(x2, g, b, w16, *tabs)


DIL_BQ = 128
DIL_WIN = 256
DIL_GROUP = 8
DIL_BASE = 4
DIL_MASKS = 3


def _dil_plan(seq, window, d):
    length = seq // d
    half = window // (2 * d)
    bq = min(DIL_BQ, length)
    win = min(DIL_WIN, length)
    nblk = length // bq
    blocks = min(nblk, DIL_GROUP)
    residues = max(1, min(d, DIL_GROUP // nblk))
    assert seq % d == 0 and length % bq == 0 and nblk % blocks == 0 and d % residues == 0
    assert win >= min(length, bq + 2 * half) and half % BF16_ROWS == 0
    offsets = {q0 - min(max(q0 - half, 0), length - win) for q0 in range(0, length, bq)}
    assert offsets <= {t * half for t in range(DIL_MASKS)}
    return length, half, bq, win, nblk, blocks, residues


def _dil_attn_kernel(q_ref, k_ref, v_ref, g_ref, o_ref,
                     stage, q4, k4, v4, qd, kd, vd, vaug, acc, m_s, l_s, bias, *, seq):
    scale = HEAD_DIM ** -0.5
    quarter = seq // DIL_BASE
    for src, dst in ((q_ref, q4), (k_ref, k4), (v_ref, v4)):
        stage[...] = src[...].astype(F32)
        for r in range(DIL_BASE):
            dst[pl.ds(r * quarter, quarter), :] = stage[pl.ds(r, quarter, stride=DIL_BASE), :]
    vaug[:, :LANES] = v_ref[...]
    vaug[:, LANES:] = jnp.ones((seq, LANES), BF16)
    vd[:, LANES:] = jnp.ones((vd.shape[0], LANES), BF16)

    def run_config(first, window, d):
        length, half, bq, win, nblk, blocks, residues = _dil_plan(seq, window, d)
        row_minus_col = (lax.broadcasted_iota(jnp.int32, (bq, win), 0)
                         - lax.broadcasted_iota(jnp.int32, (bq, win), 1))
        for t in range(DIL_MASKS):
            bias[t, :bq, :win] = jnp.where(jnp.abs(t * half + row_minus_col) <= half, 0.0, NEG_BIG)

        def group(rg, bg):
            if d == 1:
                qs, ks, vs = q_ref, k_ref, vaug
            else:
                qs, ks, vs = qd, kd, vd
            items = []
            for j in range(residues):
                r = rg * residues + j
                for u in range(blocks):
                    q0 = pl.multiple_of((bg * blocks + u) * bq, bq)
                    k0 = pl.multiple_of(jnp.clip(q0 - half, 0, length - win), BF16_ROWS)
                    rows = pl.ds(q0, bq) if d == 1 else pl.ds(r + d * q0, bq, stride=d)
                    items.append((j * length, q0, k0, rows))
            scores = []
            for base, q0, k0, _ in items:
                s = _dot_nt(qs[pl.ds(base + q0, bq), :], ks[pl.ds(base + k0, win), :]) * scale
                scores.append(s + bias[(q0 - k0) // half, :bq, :win])
            m_new = [jnp.broadcast_to(jnp.max(s, -1, keepdims=True), (bq, LANES)) for s in scores]
            if not first:
                m_old = [m_s[rows, :] for _, _, _, rows in items]
                m_new = [jnp.maximum(a, b) for a, b in zip(m_old, m_new)]
            def widen(m):
                return jnp.tile(m, (1, win // LANES)) if win % LANES == 0 else m[:, :1]
            probs = [jnp.exp(s - widen(m)).astype(BF16) for s, m in zip(scores, m_new)]
            pv = [jnp.dot(p, vs[pl.ds(base + k0, win), :], preferred_element_type=F32)
                  for p, (base, _, k0, _) in zip(probs, items)]
            for idx, (_, _, _, rows) in enumerate(items):
                a_new, l_new = pv[idx][:, :LANES], pv[idx][:, LANES:]
                if not first:
                    alpha = jnp.exp(m_old[idx] - m_new[idx])
                    a_new = alpha * acc[rows, :] + a_new
                    l_new = alpha * l_s[rows, :] + l_new
                acc[rows, :] = a_new
                m_s[rows, :] = m_new[idx]
                l_s[rows, :] = l_new

        def residue_group(rg, carry):
            if d > 1:
                for j in range(residues):
                    r = rg * residues + j
                    sub = pl.ds((r % DIL_BASE) * quarter + r // DIL_BASE, length, stride=d // DIL_BASE)
                    dst = pl.ds(j * length, length)
                    qd[dst, :] = q4[sub, :].astype(BF16)
                    kd[dst, :] = k4[sub, :].astype(BF16)
                    vd[dst, :LANES] = v4[sub, :].astype(BF16)
            if nblk == blocks:
                group(rg, 0)
            else:
                lax.fori_loop(0, nblk // blocks, lambda bg, c: (group(rg, bg), c)[1], 0)
            return carry

        if d == residues:
            residue_group(0, 0)
        else:
            lax.fori_loop(0, d // residues, residue_group, 0)

    for ci, (window, d) in enumerate(sorted(DIL_CONFIGS, key=lambda wd: -wd[1])):
        run_config(ci == 0, window, d)

    o_ref[...] = _head_rms(acc[...] / l_s[...], g_ref[...]).astype(o_ref.dtype)


def _dil_attn(proj3, g):
    bsz, seq, _ = proj3.shape
    assert all(d == 1 or d % DIL_BASE == 0 for _, d in DIL_CONFIGS) and seq % (DIL_BASE * SUBLANES) == 0
    plans = [_dil_plan(seq, window, d) for window, d in DIL_CONFIGS if d > 1]
    sub_rows = max(length * residues for length, _, _, _, _, _, residues in plans)

    def spec(col0):
        return pl.BlockSpec((None, seq, LANES), lambda b, h: (b, 0, col0 + h))

    return pl.pallas_call(
        functools.partial(_dil_attn_kernel, seq=seq),
        grid=(bsz, N_DIL_HEADS),
        in_specs=[spec(COL_QA), spec(COL_KA), spec(COL_VA), pl.BlockSpec((1, LANES), lambda b, h: (0, 0))],
        out_specs=pl.BlockSpec((None, seq, LANES), lambda b, h: (b, 0, h)),
        out_shape=jax.ShapeDtypeStruct((bsz, seq, DIL_WIDTH), BF16),
        scratch_shapes=[pltpu.VMEM((seq, LANES), F32)] * 4
        + [pltpu.VMEM((sub_rows, LANES), BF16)] * 2
        + [pltpu.VMEM((sub_rows, 2 * LANES), BF16), pltpu.VMEM((seq, 2 * LANES), BF16)]
        + [pltpu.VMEM((seq, LANES), F32)] * 3
        + [pltpu.VMEM((DIL_MASKS, DIL_BQ, DIL_WIN), F32)],
        compiler_params=_cparams("parallel", "parallel"),
        name="dil_attn",
    )(proj3, proj3, proj3, g)


DIFF_TQ = 512
DIFF_KEY_CHUNK = 1024


def _diff_attn_kernel(q_ref, k_ref, v_ref, lq1, lk1, lq2, lk2, g_ref, o_ref, vaug, s_scr, mx_scr, *, lam_init):
    i = pl.program_id(2)
    last = pl.num_programs(2) - 1
    seq = k_ref.shape[0]
    kc = min(DIFF_KEY_CHUNK, seq)
    assert seq % kc == 0
    chunks = [pl.ds(c * kc, kc) for c in range(seq // kc)]
    scale = DIFF_QK_DIM ** -0.5
    assert 2.0 ** round(math.log2(scale)) == scale

    def step(finish, score):
        if score:
            q = q_ref[...] * scale
            lo = lax.broadcasted_iota(jnp.int32, q.shape, 1) < DIFF_QK_DIM
            zero = jnp.zeros_like(q)
            q_maps = (jnp.where(lo, q, zero), jnp.where(lo, zero, q))
        outs = []
        for mp in range(2):
            pv, m_next = None, None
            if finish:
                m_prev = mx_scr[mp][:, :1]
            for c in chunks:
                if finish:
                    p = jnp.exp(s_scr[mp, :, c] - m_prev).astype(BF16)
                    part = jnp.dot(p, vaug[c, :], preferred_element_type=F32)
                    pv = part if pv is None else pv + part
                if score:
                    s_new = _dot_nt(q_maps[mp], k_ref[c, :])
                    s_scr[mp, :, c] = s_new
                    m_chunk = jnp.max(s_new, -1, keepdims=True)
                    m_next = m_chunk if m_next is None else jnp.maximum(m_next, m_chunk)
            if score:
                mx_scr[mp] = jnp.broadcast_to(m_next, mx_scr.shape[1:])
            if finish:
                outs.append(pv[:, :LANES] / pv[:, LANES:])
        if finish:
            o0, o1 = outs
            lam = (jnp.exp(jnp.sum(lq1[...] * lk1[...], keepdims=True))
                   - jnp.exp(jnp.sum(lq2[...] * lk2[...], keepdims=True)) + lam_init)
            a = o0 - lam * o1
            o_ref[...] = (_head_rms(a, g_ref[...]) * (1.0 - lam_init)).astype(o_ref.dtype)

    @pl.when(i == 0)
    def _():
        vaug[:, :LANES] = v_ref[...]
        vaug[:, LANES:] = jnp.ones((seq, LANES), BF16)
        step(finish=False, score=True)

    @pl.when(jnp.logical_and(i > 0, i < last))
    def _():
        step(finish=True, score=True)

    @pl.when(i == last)
    def _():
        step(finish=True, score=False)


def _diff_attn(proj3, lq1, lk1, lq2, lk2, g, lam_init):
    bsz, seq, _ = proj3.shape
    tq = DIFF_TQ
    assert seq % tq == 0
    n_tiles = seq // tq
    vec = pl.BlockSpec((1, DIFF_QK_DIM), lambda b, h, i: (0, 0))
    return pl.pallas_call(
        functools.partial(_diff_attn_kernel, lam_init=lam_init),
        grid=(bsz, N_DIFF_HEADS, n_tiles + 1),
        in_specs=[
            pl.BlockSpec((None, tq, LANES), lambda b, h, i: (b, jnp.minimum(i, n_tiles - 1), COL_QD + h)),
            pl.BlockSpec((None, seq, LANES), lambda b, h, i: (b, 0, COL_KD + h)),
            pl.BlockSpec((None, seq, LANES), lambda b, h, i: (b, 0, COL_VD + h)),
            vec, vec, vec, vec,
            pl.BlockSpec((1, DIFF_V_DIM), lambda b, h, i: (0, 0)),
        ],
        out_specs=pl.BlockSpec((None, tq, LANES), lambda b, h, i: (b, jnp.maximum(i - 1, 0), h)),
        out_shape=jax.ShapeDtypeStruct((bsz, seq, DIFF_WIDTH), BF16),
        scratch_shapes=[pltpu.VMEM((seq, 2 * LANES), BF16), pltpu.VMEM((2, tq, seq), F32),
                        pltpu.VMEM((2, tq, LANES), F32)],
        compiler_params=_cparams("parallel", "parallel", "arbitrary"),
        name="diff_attn",
    )(proj3, proj3, proj3, lq1, lk1, lq2, lk2, g)


OUTPROJ_TM = 256


def _outproj_kernel(ma_ref, md_ref, wa_ref, wd_ref, h_ref, g_ref, b_ref, o32_ref, o16_ref, o16t_ref, raw_ref):
    @pl.when(pl.program_id(0) == 0)
    def _():
        raw_ref[...] = jnp.zeros_like(raw_ref)

    h = _layer_norm(DEEPNORM_ALPHA * h_ref[...] + raw_ref[...], g_ref[...], b_ref[...])
    o32_ref[...] = h
    o16_ref[...] = h.astype(BF16)
    o16t_ref[...] = h.T.astype(BF16)

    mix = jnp.dot(ma_ref[...], wa_ref[...], preferred_element_type=F32)
    raw_ref[...] = mix + jnp.dot(md_ref[...], wd_ref[...], preferred_element_type=F32)


def _outproj(mix_a, mix_d, w_a, w_d, h0, g, b):
    t = h0.shape[0]
    tm = OUTPROJ_TM
    assert t % tm == 0
    n_tiles = t // tm
    cur = lambda s: jnp.minimum(s, n_tiles - 1)
    prev = lambda s: jnp.maximum(s - 1, 0)
    row = pl.BlockSpec((1, D_MODEL), lambda s: (0, 0))
    return pl.pallas_call(
        _outproj_kernel,
        grid=(n_tiles + 1,),
        in_specs=[
            pl.BlockSpec((tm, DIL_WIDTH), lambda s: (cur(s), 0)),
            pl.BlockSpec((tm, DIFF_WIDTH), lambda s: (cur(s), 0)),
            pl.BlockSpec((DIL_WIDTH, D_MODEL), lambda s: (0, 0)),
            pl.BlockSpec((DIFF_WIDTH, D_MODEL), lambda s: (0, 0)),
            pl.BlockSpec((tm, D_MODEL), lambda s: (prev(s), 0)),
            row, row,
        ],
        out_specs=[pl.BlockSpec((tm, D_MODEL), lambda s: (prev(s), 0))] * 2
        + [pl.BlockSpec((D_MODEL, tm), lambda s: (0, prev(s)))],
        out_shape=[jax.ShapeDtypeStruct((t, D_MODEL), F32), jax.ShapeDtypeStruct((t, D_MODEL), BF16),
                   jax.ShapeDtypeStruct((D_MODEL, t), BF16)],
        scratch_shapes=[pltpu.VMEM((tm, D_MODEL), F32)],
        compiler_params=_cparams("arbitrary"),
        name="outproj",
    )(mix_a, mix_d, w_a, w_d, h0, g, b)


SCORE_TM = 512
N_KEYSETS = 2 * PEER_HEADS
HALF_KEY_DIM = PEER_KEY_DIM // 2


def _peer_score_kernel(h_ref, wq_ref, keys_ref, s_ref):
    q = jnp.dot(h_ref[...], wq_ref[...], preferred_element_type=F32).astype(BF16)
    for hc in range(N_KEYSETS):
        qs = q[:, hc * HALF_KEY_DIM:(hc + 1) * HALF_KEY_DIM]
        s_ref[hc] = _dot_nt(keys_ref[hc], qs)


def _peer_scores(h16, wq16, keys16):
    t = h16.shape[0]
    tm = SCORE_TM
    assert t % tm == 0 and HALF_KEY_DIM == LANES
    return pl.pallas_call(
        _peer_score_kernel,
        grid=(t // tm,),
        in_specs=[
            pl.BlockSpec((tm, D_MODEL), lambda i: (i, 0)),
            pl.BlockSpec((D_MODEL, PEER_HEADS * PEER_KEY_DIM), lambda i: (0, 0)),
            pl.BlockSpec((N_KEYSETS, PEER_N_KEYS, HALF_KEY_DIM), lambda i: (0, 0, 0)),
        ],
        out_specs=pl.BlockSpec((N_KEYSETS, PEER_N_KEYS, tm), lambda i: (0, 0, i)),
        out_shape=jax.ShapeDtypeStruct((N_KEYSETS, PEER_N_KEYS, t), F32),
        compiler_params=_cparams("parallel"),
        name="peer_scores",
    )(h16, wq16, keys16)


ROUTE_TT = 1024


SUBLANES = 8


def _sort_network(n):
    def merge(lo, hi, r):
        step = r * 2
        if step < hi - lo:
            yield from merge(lo, hi, step)
            yield from merge(lo + r, hi, step)
            yield from ((i, i + r) for i in range(lo + r, hi - r, step))
        else:
            yield (lo, lo + r)

    def sort(lo, hi):
        if hi - lo >= 1:
            mid = lo + (hi - lo) // 2
            yield from sort(lo, mid)
            yield from sort(mid + 1, hi)
            yield from merge(lo, hi, 1)

    return list(sort(0, n - 1))


def _bitonic_merge_network(n):
    pairs, j = [], n // 2
    while j >= 1:
        pairs += [(i, i ^ j) for i in range(n) if i ^ j > i]
        j //= 2
    return pairs


def _compare_exchange_desc(xs, pairs):
    xs = list(xs)
    for lo, hi in pairs:
        xs[lo], xs[hi] = jnp.maximum(xs[lo], xs[hi]), jnp.minimum(xs[lo], xs[hi])
    return xs


def _top_sorted(tiles):
    k = len(tiles)
    top = _compare_exchange_desc(tiles, _sort_network(k))
    shift = 1
    while shift < SUBLANES:
        other = [pltpu.roll(x, shift, 0) for x in top]
        top = [jnp.maximum(top[i], other[k - 1 - i]) for i in range(k)]
        top = _compare_exchange_desc(top, _bitonic_merge_network(k))
        shift *= 2
    return top


def _count_greater(x, top):
    assert len(top) == 16
    b3 = top[7] > x
    b2 = jnp.where(b3, top[11], top[3]) > x
    b1 = jnp.where(b3, jnp.where(b2, top[13], top[9]), jnp.where(b2, top[5], top[1])) > x
    q = [jnp.where(b1, top[4 * i + 2], top[4 * i]) for i in range(4)]
    b0 = jnp.where(b3, jnp.where(b2, q[3], q[2]), jnp.where(b2, q[1], q[0])) > x
    count = (jnp.where(b3, 8.0, 0.0) + jnp.where(b2, 4.0, 0.0)) + (jnp.where(b1, 2.0, 0.0) + jnp.where(b0, 1.0, 0.0))
    return jnp.where(top[15] > x, 16.0, count)


def _top_ranks(v, k):
    n = v.shape[0]
    rows = lax.broadcasted_iota(jnp.int32, v.shape, 0)
    rank = jnp.full(v.shape, k, jnp.int32)
    work = v
    vals = []
    for r in range(k):
        m = jnp.max(work, axis=0, keepdims=True)
        first = jnp.min(jnp.where(work == m, rows, n), axis=0, keepdims=True)
        sel = rows == first
        rank = jnp.where(sel, r, rank)
        work = jnp.where(sel, -jnp.inf, work)
        vals.append(m)
    return jnp.concatenate(vals, axis=0), rank


def _staircase(sv0, sv1, k):
    arow = lax.broadcasted_iota(jnp.int32, sv0.shape, 0)
    count = jnp.zeros(sv0.shape, jnp.int32)
    front = sv0 + sv1[0:1]
    best = front[0:1]
    z = jnp.zeros_like(best)
    for _ in range(k):
        m = jnp.max(front, axis=0, keepdims=True)
        first = jnp.min(jnp.where(front == m, arow, k), axis=0, keepdims=True)
        win = arow == first
        z = z + jnp.exp(m - best)
        count = count + win.astype(jnp.int32)
        nw = jnp.sum(jnp.where(win, count, 0), axis=0, keepdims=True)
        nxt = jnp.sum(jnp.where(arow == nw, sv1, 0.0), axis=0, keepdims=True)
        nxt = jnp.where(nw < k, nxt, -jnp.inf)
        front = jnp.where(win, sv0 + nxt, front)
    return count.astype(F32), z


def _peer_route_kernel(s_ref, n_ref, e0_ref, r1_ref, e1_ref, sv0_s, rank0_s, sv1_s, rank1_s, cnt_s, zinv_s):
    k = PEER_TOPK
    n_keys, tt = s_ref.shape[1], s_ref.shape[2]
    n_tiles = n_keys // SUBLANES
    assert n_tiles == k
    sides = ((sv0_s, rank0_s), (sv1_s, rank1_s))

    def lane_col(c):
        return pl.ds(pl.multiple_of(c * LANES, LANES), LANES)

    def tiles_of(side, col):
        return [s_ref[side, g * SUBLANES:(g + 1) * SUBLANES, col] for g in range(n_tiles)]

    def sort_col(c, tied):
        col = lane_col(c)
        for side, (sv_s, _) in enumerate(sides):
            tiles = tiles_of(side, col)
            top = _top_sorted(tiles)
            for r in range(k):
                sv_s[r:r + 1, col] = top[r][0:1]
            equal_pair = functools.reduce(jnp.logical_or, [top[r] == top[r + 1] for r in range(k - 1)])
            at_least = sum(jnp.where(x >= top[k - 1], 1.0, 0.0) for x in tiles)
            bad = jnp.where(equal_pair, 1.0, 0.0) + jnp.abs(jnp.sum(at_least, axis=0, keepdims=True) - k)
            tied = jnp.maximum(tied, jnp.max(bad))
            if side == 1:
                for g, x in enumerate(tiles):
                    r1_ref[g * SUBLANES:(g + 1) * SUBLANES, col] = _count_greater(x, top).astype(r1_ref.dtype)
        return tied

    tied = lax.fori_loop(0, tt // LANES, sort_col, jnp.zeros((), F32)) > 0

    @pl.when(tied)
    def _():
        for side, (sv_s, rank_s) in enumerate(sides):
            sv, rank = _top_ranks(s_ref[side], k)
            sv_s[...] = sv
            rank_s[...] = rank
        r1_ref[...] = rank1_s[...].astype(F32).astype(r1_ref.dtype)

    count, z = _staircase(sv0_s[...], sv1_s[...], k)
    cnt_s[...] = count
    zinv_s[...] = 1.0 / z

    def emit_col(c, carry):
        col = lane_col(c)
        s0 = s_ref[0, :, col]
        e0_ref[:, col] = jnp.exp(s0 - sv0_s[0:1, col])
        e1_ref[:, col] = (jnp.exp(s_ref[1, :, col] - sv1_s[0:1, col]) * zinv_s[:, col]).astype(e1_ref.dtype)
        n_of_row = jnp.zeros(s0.shape, F32)
        for a in range(k):
            n_of_row = jnp.where(s0 == sv0_s[a:a + 1, col], cnt_s[a:a + 1, col], n_of_row)
        n_ref[:, col] = n_of_row
        return carry

    lax.fori_loop(0, tt // LANES, emit_col, 0)

    @pl.when(tied)
    def _():
        rank0 = rank0_s[...]
        cnt = cnt_s[...]
        n_of_row = jnp.zeros(rank0.shape, F32)
        for a in range(k):
            n_of_row = jnp.where(rank0 == a, cnt[a:a + 1], n_of_row)
        n_ref[...] = n_of_row


def _peer_route(scores):
    t = scores.shape[-1]
    tt = ROUTE_TT
    assert t % tt == 0 and tt % LANES == 0
    out_spec = pl.BlockSpec((None, PEER_N_KEYS, tt), lambda h, i: (h, 0, i))
    shape = (PEER_HEADS, PEER_N_KEYS, t)
    return pl.pallas_call(
        _peer_route_kernel,
        grid=(PEER_HEADS, t // tt),
        in_specs=[pl.BlockSpec((2, PEER_N_KEYS, tt), lambda h, i: (h, 0, i))],
        out_specs=[out_spec] * 4,
        out_shape=[jax.ShapeDtypeStruct(shape, F32), jax.ShapeDtypeStruct(shape, F32),
                   jax.ShapeDtypeStruct(shape, BF16), jax.ShapeDtypeStruct(shape, BF16)],
        scratch_shapes=[pltpu.VMEM((PEER_TOPK, tt), F32), pltpu.VMEM((PEER_N_KEYS, tt), jnp.int32),
                        pltpu.VMEM((PEER_TOPK, tt), F32), pltpu.VMEM((PEER_N_KEYS, tt), jnp.int32),
                        pltpu.VMEM((PEER_TOPK, tt), F32), pltpu.VMEM((1, tt), F32)],
        compiler_params=_cparams("parallel", "parallel"),
        name="peer_route",
    )(scores)


EXPERT_TM = 512
EXPERT_TE = 1024
EXPERT_VMEM_LIMIT = 62 * 1024 * 1024
EXPERT_DRAIN_STEPS = 2
INV_SQRT2 = 1.0 / math.sqrt(2.0)


def _expert_work(s, lag, n_work):
    return jnp.clip(s - lag, 0, n_work - 1)


def _peer_expert_kernel(xt_ref, u_ref, vt_ref, n_ref, e0_ref, r1_ref, e1_ref, h_ref, g_ref, b_ref,
                        o_ref, acc_ref, pre_ref, hid_ref, *, n_blocks, n_work):
    s = pl.program_id(0)
    te = u_ref.shape[0]
    tm = xt_ref.shape[1]
    rows_per_block = te // PEER_N_KEYS
    zero = jnp.zeros((), BF16)
    block_down = _expert_work(s, 2, n_work) % n_blocks
    block_gate = _expert_work(s, 1, n_work) % n_blocks

    @pl.when(s == 0)
    def _():
        pre_ref[...] = jnp.zeros_like(pre_ref)
        hid_ref[...] = jnp.zeros_like(hid_ref)

    @pl.when(block_down == 0)
    def _():
        acc_ref[...] = jnp.zeros_like(acc_ref)

    acc_ref[...] += jnp.dot(vt_ref[...], hid_ref[...], preferred_element_type=F32)

    a = pre_ref[...]
    act = (0.5 * a * (1.0 + lax.erf(a * INV_SQRT2))).astype(BF16)
    first_row = block_gate * rows_per_block
    gates = []
    for ib in range(rows_per_block):
        i = first_row + ib
        w = jnp.zeros((PEER_N_KEYS, tm), BF16)
        for hd in range(PEER_HEADS):
            n_b = jnp.broadcast_to(n_ref[hd, pl.ds(i, 1), :], (PEER_N_KEYS, tm)).astype(BF16)
            e0_b = jnp.broadcast_to(e0_ref[hd, pl.ds(i, 1), :], (PEER_N_KEYS, tm)).astype(BF16)
            w = w + jnp.where(r1_ref[hd] < n_b, e1_ref[hd], zero) * e0_b
        gates.append(w)
    hid_ref[...] = jnp.concatenate(gates, axis=0) * act

    pre_ref[...] = jnp.dot(u_ref[...], xt_ref[...], preferred_element_type=F32)

    @pl.when(jnp.logical_and(block_down == n_blocks - 1, s >= EXPERT_DRAIN_STEPS))
    def _():
        y = DEEPNORM_ALPHA * h_ref[...] + acc_ref[...].T
        o_ref[...] = _layer_norm(y, g_ref[...], b_ref[...])


def _peer_experts(h16t, u16, v16t, n_t, e0_t, r1_t, e1_t, h32, g, b):
    t = h16t.shape[1]
    tm, te = EXPERT_TM, EXPERT_TE
    assert t % tm == 0 and PEER_N_EXPERTS % te == 0 and te % PEER_N_KEYS == 0
    n_blocks = PEER_N_EXPERTS // te
    n_work = (t // tm) * n_blocks

    def item(lag):
        return lambda s: divmod(_expert_work(s, lag, n_work), n_blocks)

    up, gate, down = item(0), item(1), item(2)
    route = pl.BlockSpec((PEER_HEADS, PEER_N_KEYS, tm), lambda s: (0, 0, gate(s)[0]))
    row = pl.BlockSpec((1, D_MODEL), lambda s: (0, 0))
    return pl.pallas_call(
        functools.partial(_peer_expert_kernel, n_blocks=n_blocks, n_work=n_work),
        grid=(n_work + EXPERT_DRAIN_STEPS,),
        in_specs=[
            pl.BlockSpec((D_MODEL, tm), lambda s: (0, up(s)[0])),
            pl.BlockSpec((te, D_MODEL), lambda s: (up(s)[1], 0)),
            pl.BlockSpec((None, D_MODEL, te), lambda s: (down(s)[1], 0, 0)),
            route, route, route, route,
            pl.BlockSpec((tm, D_MODEL), lambda s: (down(s)[0], 0), pipeline_mode=pl.Buffered(1)),
            row, row,
        ],
        out_specs=pl.BlockSpec((tm, D_MODEL), lambda s: (down(s)[0], 0)),
        out_shape=jax.ShapeDtypeStruct((t, D_MODEL), F32),
        scratch_shapes=[pltpu.VMEM((D_MODEL, tm), F32), pltpu.VMEM((te, tm), F32), pltpu.VMEM((te, tm), BF16)],
        compiler_params=_cparams("arbitrary", vmem_limit=EXPERT_VMEM_LIMIT),
        name="peer_experts",
    )(h16t, u16, v16t, n_t, e0_t, r1_t, e1_t, h32, g, b)


def kernel(x, ln_emb_g, ln_emb_b, w_in, dil_norm_g, lambda_q1, lambda_k1, lambda_q2, lambda_k2, subln_g,
           w_out, ln1_g, ln1_b, peer_w_query, peer_sub_keys, peer_u, peer_v, ln2_g, ln2_b):
    bsz, seq, d_model = x.shape
    assert d_model == D_MODEL and w_in.shape[0] == DEPTH == 1
    t = bsz * seq
    row = lambda p: p.reshape(1, -1).astype(F32)
    lam_init = 0.8 - 0.6 * math.exp(-0.3 * 0)

    h0, proj = _inproj(x.reshape(t, D_MODEL), row(ln_emb_g), row(ln_emb_b), w_in[0].astype(BF16), seq)
    proj3 = proj.reshape(bsz, seq, IN_WIDTH)

    mix_a = _dil_attn(proj3, row(dil_norm_g[0])).reshape(t, DIL_WIDTH)
    mix_d = _diff_attn(proj3, row(lambda_q1[0]), row(lambda_k1[0]), row(lambda_q2[0]), row(lambda_k2[0]),
                       row(subln_g[0]), lam_init).reshape(t, DIFF_WIDTH)

    w_out16 = w_out[0].astype(BF16)
    h1, h1_16, h1_16t = _outproj(mix_a, mix_d, w_out16[:DIL_WIDTH], w_out16[DIL_WIDTH:], h0,
                                 row(ln1_g[0]), row(ln1_b[0]))

    keys16 = peer_sub_keys[0].reshape(N_KEYSETS, PEER_N_KEYS, HALF_KEY_DIM).astype(BF16)
    scores = _peer_scores(h1_16, peer_w_query[0].astype(BF16), keys16)
    n_t, e0_t, r1_t, e1_t = _peer_route(scores)
    v16t = peer_v[0].astype(BF16).reshape(PEER_N_EXPERTS // EXPERT_TE, EXPERT_TE, D_MODEL).transpose(0, 2, 1)
    out = _peer_experts(h1_16t, peer_u[0].astype(BF16), v16t, n_t, e0_t, r1_t, e1_t,
                        h1, row(ln2_g[0]), row(ln2_b[0]))
    return out.reshape(bsz, seq, D_MODEL)
```
